```python
import math
import jax
import jax.numpy as jnp
from jax import lax
import numpy as np

D_MODEL = 1024
BATCH = 1
SEQ = 16384
DEPTH = 2
DEC_BATCH = 32
DEC_SEQ = 2048
PAST_LEN = 128

GRID_W = 64
HEAD_DIM = 64
EPS = 1e-6

A_HEADS = D_MODEL // (2 * HEAD_DIM)
A_KV_HEADS = 2
A_GROUP = A_HEADS // A_KV_HEADS
ROPE_THETA = 10000.0
Q_BLOCK = 128

DILATED_PAIRS = ((128, 1), (512, 4), (2048, 16))
B_GROUPS = len(DILATED_PAIRS)
B_HEADS = D_MODEL // (4 * HEAD_DIM)

C_HEADS = D_MODEL // (4 * HEAD_DIM)
NA_ROWS = 8
NA_COLS = 16

N_BUCKETS = 32
MAX_DISTANCE = 2048

N_EXPERTS = 16
EC_CAPACITY = 2
D_EXPERT = 2 * D_MODEL

A_Q = A_HEADS * HEAD_DIM
A_KV = A_KV_HEADS * HEAD_DIM
B_W = B_GROUPS * B_HEADS * HEAD_DIM
B_OUT = B_HEADS * HEAD_DIM
C_W = C_HEADS * HEAD_DIM
IN_SPLITS = (A_Q, A_KV, A_KV, B_W, B_W, B_W, C_W, C_W, C_W)
D_IN = sum(IN_SPLITS)
D_MIX = A_Q + B_OUT + C_W

kernel_name = "hybrid_bidir_encoder_gqa_dilated_na_ecmoe"


def rms_norm(x, gain):
    xf = x.astype(jnp.float32)
    y = xf * lax.rsqrt(jnp.mean(xf * xf, axis=-1, keepdims=True) + EPS)
    return (y * gain.astype(jnp.float32)).astype(x.dtype)


def axial_rope_tables(S):
    t = jnp.arange(S)
    row = (t // GRID_W).astype(jnp.float32)
    col = (t % GRID_W).astype(jnp.float32)
    half = HEAD_DIM // 2
    freqs = ROPE_THETA ** (-jnp.arange(0, half, 2, dtype=jnp.float32) / half)
    ang = jnp.concatenate([row[:, None] * freqs, col[:, None] * freqs], axis=-1)
    return jnp.cos(ang), jnp.sin(ang)


def apply_rope(x, cos, sin):
    xf = x.astype(jnp.float32).reshape(x.shape[:-1] + (HEAD_DIM // 2, 2))
    x1, x2 = xf[..., 0], xf[..., 1]
    c = cos[None, :, None, :]
    s = sin[None, :, None, :]
    out = jnp.stack([x1 * c - x2 * s, x1 * s + x2 * c], axis=-1).reshape(x.shape)
    return out.astype(x.dtype)


def t5_bucket(rel):
    nb = N_BUCKETS // 2
    max_exact = nb // 2
    ret = (rel > 0).astype(jnp.int32) * nb
    n = jnp.abs(rel)
    large = max_exact + (jnp.log(jnp.maximum(n, 1).astype(jnp.float32) / max_exact)
                         / math.log(MAX_DISTANCE / max_exact) * (nb - max_exact)).astype(jnp.int32)
    large = jnp.minimum(large, nb - 1)
    return ret + jnp.where(n < max_exact, n, large)


def mixer_a(q, k, v, q_gain, k_gain):
    B, S = q.shape[:2]
    cos, sin = axial_rope_tables(S)
    q = apply_rope(rms_norm(q, q_gain), cos, sin)
    k = apply_rope(rms_norm(k, k_gain), cos, sin)
    nblk = S // Q_BLOCK
    qb = q.reshape(B, nblk, Q_BLOCK, A_KV_HEADS, A_GROUP, HEAD_DIM).transpose(1, 0, 2, 3, 4, 5)
    scale = HEAD_DIM ** -0.5

    def block(qi):
        s = jnp.einsum('bqkgd,bskd->bkgqs', qi, k, preferred_element_type=jnp.float32) * scale
        p = jax.nn.softmax(s, axis=-1)
        return jnp.einsum('bkgqs,bskd->bqkgd', p.astype(v.dtype), v)

    o = lax.map(block, qb)
    return o.transpose(1, 0, 2, 3, 4, 5).reshape(B, S, A_Q)


def dilated_branch(q, k, v, bias_table, window, dilation):
    B, S, H, D = q.shape
    R = window // (2 * dilation)
    L = S // dilation
    nb = -(-L // R)
    Lp = nb * R

    def to_sub(x):
        return x.reshape(B, L, dilation, H, D).transpose(0, 2, 1, 3, 4)

    qs = jnp.pad(to_sub(q), ((0, 0), (0, 0), (0, Lp - L), (0, 0), (0, 0))).reshape(B, dilation, nb, R, H, D)

    def windows(x):
        xp = jnp.pad(to_sub(x), ((0, 0), (0, 0), (R, Lp - L + R), (0, 0), (0, 0))).reshape(B, dilation, nb + 2, R, H, D)
        return jnp.concatenate([xp[:, :, :-2], xp[:, :, 1:-1], xp[:, :, 2:]], axis=3)

    kw = windows(k)
    vw = windows(v)
    off = jnp.arange(3 * R)[None, :] - R - jnp.arange(R)[:, None]
    bias = bias_table[t5_bucket(off * dilation)].transpose(2, 0, 1)
    key_sub = jnp.arange(nb)[:, None] * R - R + jnp.arange(3 * R)[None, :]
    valid = ((jnp.abs(off) <= R)[None] & (key_sub[:, None, :] >= 0) & (key_sub[:, None, :] < L))
    s = jnp.einsum('brnqhd,brnkhd->brnhqk', qs, kw, preferred_element_type=jnp.float32) * (HEAD_DIM ** -0.5)
    s = jnp.where(valid[None, None, :, None], s + bias[None, None, None], -jnp.inf)
    m = jnp.max(s, axis=-1, keepdims=True)
    p = jnp.exp(s - m)
    den = jnp.sum(p, axis=-1)
    o = jnp.einsum('brnhqk,brnkhd->brnqhd', p.astype(vw.dtype), vw).astype(jnp.float32)
    o = o / den.transpose(0, 1, 2, 4, 3)[..., None]
    lse = (m[..., 0] + jnp.log(den)).transpose(0, 1, 2, 4, 3)
    o = o.reshape(B, dilation, Lp, H, D)[:, :, :L].transpose(0, 2, 1, 3, 4).reshape(B, S, H, D)
    lse = lse.reshape(B, dilation, Lp, H)[:, :, :L].transpose(0, 2, 1, 3).reshape(B, S, H)
    return o, lse


def mixer_b(q, k, v, rel_bias):
    B, S = q.shape[:2]
    outs, lses = [], []
    for g, (window, dilation) in enumerate(DILATED_PAIRS):
        o, lse = dilated_branch(q[:, :, g], k[:, :, g], v[:, :, g],
                                rel_bias[:, g * B_HEADS:(g + 1) * B_HEADS], window, dilation)
        outs.append(o)
        lses.append(lse)
    w = jax.nn.softmax(jnp.stack(lses, axis=0), axis=0)
    o = jnp.sum(w[..., None] * jnp.stack(outs, axis=0), axis=0)
    return o.reshape(B, S, B_OUT).astype(q.dtype)


def mixer_c(q, k, v, rpb):
    B, S, H, D = q.shape
    rows = S // GRID_W
    kr = min(NA_ROWS, rows)
    qg = q.reshape(B, rows, GRID_W, H, D)
    kg = k.reshape(B, rows, GRID_W, H, D)
    vg = v.reshape(B, rows, GRID_W, H, D)
    r = jnp.arange(rows)
    rs = jnp.clip(r - kr // 2, 0, rows - kr)
    row_idx = rs[:, None] + jnp.arange(kr)[None, :]
    kwin = kg[:, row_idx]
    vwin = vg[:, row_idx]
    c = jnp.arange(GRID_W)
    cs = jnp.clip(c - NA_COLS // 2, 0, GRID_W - NA_COLS)
    col_ok = (c[None, :] >= cs[:, None]) & (c[None, :] < cs[:, None] + NA_COLS)
    drow = row_idx - r[:, None]
    dcol = jnp.clip(c[None, :] - c[:, None], -(NA_COLS - 1), NA_COLS - 1)
    bias = rpb[:, drow[:, None, :, None] + NA_ROWS - 1, dcol[None, :, None, :] + NA_COLS - 1]
    s = jnp.einsum('brqhd,brkwhd->bhrqkw', qg, kwin, preferred_element_type=jnp.float32) * (HEAD_DIM ** -0.5)
    s = jnp.where(col_ok[:, None, :], s + bias[None].astype(jnp.float32), -jnp.inf)
    p = jax.nn.softmax(s.reshape(s.shape[:4] + (kr * GRID_W,)), axis=-1).reshape(s.shape)
    o = jnp.einsum('bhrqkw,brkwhd->brqhd', p.astype(vwin.dtype), vwin)
    return o.reshape(B, S, C_W)


def expert_choice_ffn(h, w_router, w_gate, w_up, w_down):
    B, S, Dm = h.shape
    n_tok = B * S
    cap = EC_CAPACITY * n_tok // N_EXPERTS
    tok = h.reshape(n_tok, Dm)
    aff = jax.nn.softmax(jnp.dot(tok.astype(jnp.float32), w_router.astype(jnp.float32)), axis=-1)
    gate, idx = lax.top_k(aff.T, cap)

    def expert(args):
        idx_e, gate_e, wg, wu, wd = args
        xe = tok[idx_e]
        hid = jax.nn.silu(xe @ wg) * (xe @ wu)
        return (hid @ wd) * gate_e[:, None].astype(h.dtype)

    ye = lax.map(expert, (idx, gate, w_gate, w_up, w_down))
    out = jnp.zeros((n_tok, Dm), h.dtype).at[idx.reshape(-1)].add(ye.reshape(-1, Dm))
    return out.reshape(B, S, Dm)


def encoder_layer(x, w_in, w_out, norm_mix, norm_ffn, q_gain, k_gain, out_gain,
                  na_rpb, rel_bias, w_router, w_gate, w_up, w_down):
    B, S, _ = x.shape
    h = rms_norm(x, norm_mix)
    proj = jnp.einsum('bsd,de->bse', h, w_in)
    a_q, a_k, a_v, b_q, b_k, b_v, c_q, c_k, c_v = jnp.split(
        proj, np.cumsum(IN_SPLITS)[:-1].tolist(), axis=-1)
    o_a = mixer_a(a_q.reshape(B, S, A_HEADS, HEAD_DIM), a_k.reshape(B, S, A_KV_HEADS, HEAD_DIM),
                  a_v.reshape(B, S, A_KV_HEADS, HEAD_DIM), q_gain, k_gain)
    bshape = (B, S, B_GROUPS, B_HEADS, HEAD_DIM)
    o_b = mixer_b(b_q.reshape(bshape), b_k.reshape(bshape), b_v.reshape(bshape), rel_bias)
    cshape = (B, S, C_HEADS, HEAD_DIM)
    o_c = mixer_c(c_q.reshape(cshape), c_k.reshape(cshape), c_v.reshape(cshape), na_rpb)
    g_a, g_b, g_c = jnp.split(out_gain, [A_Q, A_Q + B_OUT])
    mixed = jnp.concatenate([rms_norm(o_a, g_a), rms_norm(o_b, g_b), rms_norm(o_c, g_c)], axis=-1)
    x = x + jnp.einsum('bse,ed->bsd', mixed, w_out)
    x = x + expert_choice_ffn(rms_norm(x, norm_ffn), w_router, w_gate, w_up, w_down)
    return x


def run_trunk(x, w_in, w_out, norm_mix, norm_ffn, q_gain, k_gain, out_gain, na_rpb,
              rel_bias, w_router, w_gate, w_up, w_down, final_norm):
    for l in range(DEPTH):
        x = encoder_layer(x, w_in[l], w_out[l], norm_mix[l], norm_ffn[l], q_gain[l], k_gain[l],
                          out_gain[l], na_rpb[l], rel_bias, w_router[l], w_gate[l], w_up[l], w_down[l])
    return rms_norm(x, final_norm)


def setup_inputs(seed: int = 0) -> dict:
    key = jax.random.key(seed)
    ks = jax.random.split(key, 16)

    def nrm(k, shape, scale):
        return jax.random.normal(k, shape, jnp.float32) * scale

    return {
        "x_prompt": nrm(ks[0], (BATCH, SEQ, D_MODEL), 1.0),
        "x_sample": nrm(ks[1], (DEC_BATCH, DEC_SEQ, D_MODEL), 1.0),
        "w_in": nrm(ks[2], (DEPTH, D_MODEL, D_IN), D_MODEL ** -0.5),
        "w_out": nrm(ks[3], (DEPTH, D_MIX, D_MODEL), D_MIX ** -0.5),
        "norm_mix": 1.0 + nrm(ks[4], (DEPTH, D_MODEL), 0.01),
        "norm_ffn": 1.0 + nrm(ks[5], (DEPTH, D_MODEL), 0.01),
        "q_gain": 1.0 + nrm(ks[6], (DEPTH, HEAD_DIM), 0.01),
        "k_gain": 1.0 + nrm(ks[7], (DEPTH, HEAD_DIM), 0.01),
        "out_gain": 1.0 + nrm(ks[8], (DEPTH, D_MIX), 0.01),
        "na_rpb": nrm(ks[9], (DEPTH, C_HEADS, 2 * NA_ROWS - 1, 2 * NA_COLS - 1), 0.1),
        "rel_bias": nrm(ks[10], (N_BUCKETS, B_GROUPS * B_HEADS), 0.1),
        "w_router": nrm(ks[11], (DEPTH, D_MODEL, N_EXPERTS), D_MODEL ** -0.5),
        "w_gate": nrm(ks[12], (DEPTH, N_EXPERTS, D_MODEL, D_EXPERT), D_MODEL ** -0.5),
        "w_up": nrm(ks[13], (DEPTH, N_EXPERTS, D_MODEL, D_EXPERT), D_MODEL ** -0.5),
        "w_down": nrm(ks[14], (DEPTH, N_EXPERTS, D_EXPERT, D_MODEL), D_EXPERT ** -0.5),
        "final_norm": 1.0 + nrm(ks[15], (D_MODEL,), 0.01),
    }


def reference(x_prompt, x_sample, w_in, w_out, norm_mix, norm_ffn, q_gain, k_gain, out_gain,
              na_rpb, rel_bias, w_router, w_gate, w_up, w_down, final_norm):
    y_prompt = run_trunk(x_prompt, w_in, w_out, norm_mix, norm_ffn, q_gain, k_gain, out_gain,
                         na_rpb, rel_bias, w_router, w_gate, w_up, w_down, final_norm)
    y_sample = run_trunk(x_sample, w_in, w_out, norm_mix, norm_ffn, q_gain, k_gain, out_gain,
                         na_rpb, rel_bias, w_router, w_gate, w_up, w_down, final_norm)
    return (y_prompt, y_sample)
```

```python
import functools
import math

import jax
import jax.numpy as jnp
import numpy as np
from jax import lax
from jax.experimental import pallas as pl
from jax.experimental.pallas import tpu as pltpu

D_MODEL = 1024
DEPTH = 2
GRID_W = 64
HEAD_DIM = 64
EPS = 1e-6
A_HEADS = 8
A_KV_HEADS = 2
A_GROUP = A_HEADS // A_KV_HEADS
ROPE_THETA = 10000.0
DILATED_PAIRS = ((128, 1), (512, 4), (2048, 16))
B_GROUPS = 3
B_HEADS = 4
C_HEADS = 4
NA_ROWS = 8
NA_COLS = 16
N_BUCKETS = 32
MAX_DISTANCE = 2048
N_EXPERTS = 16
EC_CAPACITY = 2
D_EXPERT = 2 * D_MODEL
A_Q = A_HEADS * HEAD_DIM
A_KV = A_KV_HEADS * HEAD_DIM
B_OUT = B_HEADS * HEAD_DIM
C_W = C_HEADS * HEAD_DIM
D_IN = 3840
D_MIX = A_Q + B_OUT + C_W

LANES = 128
PROJ_CHUNK = 768
NEG = -1e30
LOG2E = 1.4426950408889634
VMEM_LIMIT = 56 * 1024 * 1024

F32 = jnp.float32
BF16 = jnp.bfloat16


def _cparams(sem):
    return pltpu.CompilerParams(dimension_semantics=sem, vmem_limit_bytes=VMEM_LIMIT)


def _inproj_kernel(x_ref, g_ref, w_ref, *o_refs):
    x = x_ref[...]
    ms = jnp.mean(x * x, axis=-1, keepdims=True)
    h = (x * lax.rsqrt(ms + EPS) * g_ref[...]).astype(BF16)
    for k, o_ref in enumerate(o_refs):
        o_ref[...] = jnp.dot(h, w_ref[:, k * PROJ_CHUNK:(k + 1) * PROJ_CHUNK],
                             preferred_element_type=F32).astype(BF16)


def in_proj(x, gain, w_bf16, tm=512):
    n = x.shape[0]
    nchunk = D_IN // PROJ_CHUNK
    return pl.pallas_call(
        _inproj_kernel,
        grid=(n // tm,),
        in_specs=[pl.BlockSpec((tm, D_MODEL), lambda i: (i, 0)),
                  pl.BlockSpec((1, D_MODEL), lambda i: (0, 0)),
                  pl.BlockSpec((D_MODEL, D_IN), lambda i: (0, 0))],
        out_specs=[pl.BlockSpec((tm, PROJ_CHUNK), lambda i: (i, 0))] * nchunk,
        out_shape=[jax.ShapeDtypeStruct((n, PROJ_CHUNK), BF16)] * nchunk,
        compiler_params=_cparams(("parallel",)),
        name="in_proj",
    )(x, gain.reshape(1, D_MODEL), w_bf16)


def _rope_tables(seq):
    t = np.arange(seq)
    row = (t // GRID_W).astype(np.float64)
    col = (t % GRID_W).astype(np.float64)
    half = HEAD_DIM // 2
    freqs = ROPE_THETA ** (-np.arange(0, half, 2, dtype=np.float64) / half)
    ang = np.concatenate([row[:, None] * freqs, col[:, None] * freqs], axis=-1)
    cos = np.repeat(np.cos(ang), 2, axis=-1)
    sin = np.repeat(np.sin(ang), 2, axis=-1)
    sign = np.tile(np.array([-1.0, 1.0], np.float32), HEAD_DIM // 2)
    cos2 = np.tile(cos, (1, 2)).astype(np.float32)
    sin2 = np.tile(sin * sign, (1, 2)).astype(np.float32)
    return jnp.asarray(cos2), jnp.asarray(sin2)


def _headnorm_rope(x, gain2, cos, sin, seg):
    sq = x * x
    hi = sq.astype(BF16)
    lo = (sq - hi.astype(F32)).astype(BF16)
    ms = (jnp.dot(hi, seg, preferred_element_type=F32)
          + jnp.dot(lo, seg, preferred_element_type=F32))
    y = x * lax.rsqrt(ms + EPS) * gain2
    lane = lax.broadcasted_iota(jnp.int32, y.shape, 1)
    even = (lane % 2) == 0
    swapped = jnp.where(even, pltpu.roll(y, LANES - 1, 1), pltpu.roll(y, 1, 1))
    return y * cos + swapped * sin


def _aprep_kernel(a_ref, cos_ref, sin_ref, qg_ref, kg_ref, q_ref, k_ref, v_ref):
    cos = cos_ref[...]
    sin = sin_ref[...]
    r = lax.broadcasted_iota(jnp.int32, (LANES, LANES), 0) // HEAD_DIM
    c = lax.broadcasted_iota(jnp.int32, (LANES, LANES), 1) // HEAD_DIM
    seg = jnp.where(r == c, 1.0 / HEAD_DIM, 0.0).astype(BF16)
    tm = a_ref.shape[0]
    lane = lax.broadcasted_iota(jnp.int32, (tm, LANES), 1)
    low = lane < HEAD_DIM
    qscale = (HEAD_DIM ** -0.5) * LOG2E
    for t in range(A_Q // LANES):
        x = a_ref[:, t * LANES:(t + 1) * LANES].astype(F32)
        y = _headnorm_rope(x, qg_ref[...], cos, sin, seg) * qscale
        swapped = pltpu.roll(y, HEAD_DIM, 1)
        if (2 * t) // A_GROUP == 0:
            q_even, q_odd = jnp.where(low, y, 0.0), jnp.where(low, swapped, 0.0)
        else:
            q_even, q_odd = jnp.where(low, 0.0, swapped), jnp.where(low, 0.0, y)
        q_ref[0, 2 * t] = q_even.astype(BF16)
        q_ref[0, 2 * t + 1] = q_odd.astype(BF16)
    xk = a_ref[:, A_Q:A_Q + LANES].astype(F32)
    k_ref[0] = _headnorm_rope(xk, kg_ref[...], cos, sin, seg).astype(BF16)
    xv = a_ref[:, A_Q + LANES:A_Q + 2 * LANES].astype(F32)
    one_col = (lane == HEAD_DIM).astype(F32)
    v_ref[0, 0] = (jnp.where(low, xv, 0.0) + one_col).astype(BF16)
    v_ref[0, 1] = (jnp.where(low, pltpu.roll(xv, HEAD_DIM, 1), 0.0) + one_col).astype(BF16)


def mixer_a_prep(a_arr, q_gain, k_gain, batch, seq, tm=512):
    cos2, sin2 = _rope_tables(seq)
    nb = seq // tm
    qg = jnp.tile(q_gain.astype(F32), 2).reshape(1, LANES)
    kg = jnp.tile(k_gain.astype(F32), 2).reshape(1, LANES)
    return pl.pallas_call(
        _aprep_kernel,
        grid=(batch, nb),
        in_specs=[pl.BlockSpec((tm, PROJ_CHUNK), lambda b, i: (b * nb + i, 0)),
                  pl.BlockSpec((tm, LANES), lambda b, i: (i, 0)),
                  pl.BlockSpec((tm, LANES), lambda b, i: (i, 0)),
                  pl.BlockSpec((1, LANES), lambda b, i: (0, 0)),
                  pl.BlockSpec((1, LANES), lambda b, i: (0, 0))],
        out_specs=[pl.BlockSpec((1, A_HEADS, tm, LANES), lambda b, i: (b, 0, i, 0)),
                   pl.BlockSpec((1, tm, LANES), lambda b, i: (b, i, 0)),
                   pl.BlockSpec((1, A_KV_HEADS, tm, LANES), lambda b, i: (b, 0, i, 0))],
        out_shape=[jax.ShapeDtypeStruct((batch, A_HEADS, seq, LANES), BF16),
                   jax.ShapeDtypeStruct((batch, seq, LANES), BF16),
                   jax.ShapeDtypeStruct((batch, A_KV_HEADS, seq, LANES), BF16)],
        compiler_params=_cparams(("parallel", "parallel")),
        name="mixer_a_prep",
    )(a_arr, cos2, sin2, qg, kg)


def _flash_kernel(q_ref, k_ref, v_ref, o_ref, m_ref, acc_ref):
    j = pl.program_id(3)
    tq = q_ref.shape[2]

    @pl.when(j == 0)
    def _():
        m_ref[...] = jnp.full(m_ref.shape, -jnp.inf, F32)
        acc_ref[...] = jnp.zeros(acc_ref.shape, F32)

    q = q_ref[0].reshape(A_GROUP * tq, LANES)
    s = lax.dot_general(q, k_ref[0], (((1,), (1,)), ((), ())), preferred_element_type=F32)
    m_prev = m_ref[...]
    m_new = jnp.maximum(m_prev, jnp.max(s, axis=-1, keepdims=True))
    alpha = jnp.exp2(m_prev - m_new)
    p = jnp.exp2(s - m_new).astype(BF16)
    acc_ref[...] = alpha * acc_ref[...] + jnp.dot(p, v_ref[0, 0], preferred_element_type=F32)
    m_ref[...] = m_new

    @pl.when(j == pl.num_programs(3) - 1)
    def _():
        acc = acc_ref[...]
        den = acc[:, HEAD_DIM:HEAD_DIM + 1]
        o = acc / den
        lane = lax.broadcasted_iota(jnp.int32, (tq, LANES), 1)
        low = lane < HEAD_DIM
        for t in range(A_GROUP // 2):
            o0 = o[(2 * t) * tq:(2 * t + 1) * tq]
            o1 = o[(2 * t + 1) * tq:(2 * t + 2) * tq]
            o_ref[:, t * LANES:(t + 1) * LANES] = jnp.where(
                low, o0, pltpu.roll(o1, HEAD_DIM, 1)).astype(o_ref.dtype)


def mixer_a(qx, k2, vx, batch, seq, tq=256, tk=512):
    nq = seq // tq
    grp_w = A_GROUP * HEAD_DIM
    return pl.pallas_call(
        _flash_kernel,
        grid=(batch, A_KV_HEADS, nq, seq // tk),
        in_specs=[pl.BlockSpec((1, A_GROUP, tq, LANES), lambda b, h, i, j: (b, h, i, 0)),
                  pl.BlockSpec((1, tk, LANES), lambda b, h, i, j: (b, j, 0)),
                  pl.BlockSpec((1, 1, tk, LANES), lambda b, h, i, j: (b, h, j, 0))],
        out_specs=pl.BlockSpec((tq, grp_w), lambda b, h, i, j: (b * nq + i, h)),
        out_shape=jax.ShapeDtypeStruct((batch * seq, A_Q), F32),
        scratch_shapes=[pltpu.VMEM((A_GROUP * tq, 1), F32),
                        pltpu.VMEM((A_GROUP * tq, LANES), F32)],
        compiler_params=_cparams(("parallel", "parallel", "parallel", "arbitrary")),
        name="mixer_a_flash",
    )(qx, k2, vx)


def _t5_bucket(rel):
    nb = N_BUCKETS // 2
    max_exact = nb // 2
    ret = (rel > 0).astype(jnp.int32) * nb
    n = jnp.abs(rel)
    large = max_exact + (jnp.log(jnp.maximum(n, 1).astype(jnp.float32) / max_exact)
                         / math.log(MAX_DISTANCE / max_exact) * (nb - max_exact)).astype(jnp.int32)
    large = jnp.minimum(large, nb - 1)
    return ret + jnp.where(n < max_exact, n, large)


def _dilated_bias(rel_bias_g, dilation, tl, radius):
    off = (np.arange(tl + 2 * radius)[None, :] - radius) - np.arange(tl)[:, None]
    bias = rel_bias_g[_t5_bucket(jnp.asarray(off * dilation))].transpose(2, 0, 1)
    return jnp.where(jnp.asarray(np.abs(off) <= radius)[None], bias.astype(F32), NEG)


def _head_mask(shape, h):
    lane = lax.broadcasted_iota(jnp.int32, shape, 1)
    return (lane < HEAD_DIM) if h % 2 == 0 else (lane >= HEAD_DIM)


def _windowed_heads(q, kwin, vwin, bias_ref, valid, want_lse):
    tl = q.shape[0]
    scale = jnp.asarray(HEAD_DIM ** -0.5, q.dtype)
    o_tiles, lse_tiles = [], []
    for t in range(2):
        qt = q[:, t * LANES:(t + 1) * LANES] * scale
        kt = kwin[:, t * LANES:(t + 1) * LANES]
        vt = vwin[:, t * LANES:(t + 1) * LANES]
        o_pair, lse_pair = [], []
        for hh in range(2):
            h = 2 * t + hh
            sel = _head_mask((tl, LANES), h)
            qm = jnp.where(sel, qt, jnp.zeros_like(qt))
            s = lax.dot_general(qm, kt, (((1,), (1,)), ((), ())), preferred_element_type=F32)
            s = s + bias_ref[h]
            if valid is not None:
                s = jnp.where(valid, s, NEG)
            m = jnp.max(s, axis=-1, keepdims=True)
            p = jnp.exp(s - m)
            den = jnp.sum(p, axis=-1, keepdims=True)
            pv = jnp.dot(p.astype(BF16), vt, preferred_element_type=F32)
            o_pair.append(pv / den)
            lse_pair.append(m + jnp.log(den))
        sel0 = _head_mask((tl, LANES), 0)
        o_tiles.append(jnp.where(sel0, o_pair[0], o_pair[1]))
        if want_lse:
            lse_tiles.append(jnp.where(sel0, lse_pair[0], lse_pair[1]))
    return o_tiles, lse_tiles


def _dilated_kernel(q_ref, kp_ref, km_ref, kn_ref, vp_ref, vm_ref, vn_ref, bias_ref,
                    o_ref, lse_ref, *, sub_len, radius):
    i = pl.program_id(2)
    tl = q_ref.shape[0]
    kwin = jnp.concatenate([kp_ref[...], km_ref[...], kn_ref[...]], axis=0)
    vwin = jnp.concatenate([vp_ref[...], vm_ref[...], vn_ref[...]], axis=0)
    kpos = i * tl - radius + lax.broadcasted_iota(jnp.int32, (1, tl + 2 * radius), 1)
    valid = (kpos >= 0) & (kpos < sub_len)
    o_tiles, lse_tiles = _windowed_heads(q_ref[...], kwin, vwin, bias_ref, valid, True)
    for t in range(2):
        o_ref[:, t * LANES:(t + 1) * LANES] = o_tiles[t]
        lse_ref[:, t * LANES:(t + 1) * LANES] = lse_tiles[t]


def mixer_b_group(bq, bk, bv, rel_bias, g, batch, seq, tl=128):
    window, d = DILATED_PAIRS[g]
    radius = window // (2 * d)
    sub_len = seq // d
    n = batch * seq
    assert sub_len % tl == 0 and tl % radius == 0 and radius == 64
    nblk = sub_len // tl
    hpb = tl // radius
    nhalo = sub_len // radius
    bias = _dilated_bias(rel_bias[:, g * B_HEADS:(g + 1) * B_HEADS], d, tl, radius)
    view = lambda a: a.reshape(n // d, d * PROJ_CHUNK)
    main = pl.BlockSpec((tl, B_OUT), lambda b, r, i: (b * nblk + i, r * B_GROUPS + g))
    prev = pl.BlockSpec((radius, B_OUT),
                        lambda b, r, i: (b * nhalo + jnp.maximum(i * hpb - 1, 0), r * B_GROUPS + g))
    nxt = pl.BlockSpec((radius, B_OUT),
                       lambda b, r, i: (b * nhalo + jnp.minimum((i + 1) * hpb, nhalo - 1),
                                        r * B_GROUPS + g))
    out_spec = pl.BlockSpec((tl, B_OUT), lambda b, r, i: (b * nblk + i, r))
    o, lse = pl.pallas_call(
        functools.partial(_dilated_kernel, sub_len=sub_len, radius=radius),
        grid=(batch, d, nblk),
        in_specs=[main, prev, main, nxt, prev, main, nxt,
                  pl.BlockSpec((B_HEADS, tl, tl + 2 * radius), lambda b, r, i: (0, 0, 0))],
        out_specs=[out_spec, out_spec],
        out_shape=[jax.ShapeDtypeStruct((n // d, d * B_OUT), F32)] * 2,
        compiler_params=_cparams(("parallel", "parallel", "parallel")),
        name=f"mixer_b_g{g}",
    )(view(bq), view(bk), view(bk), view(bk), view(bv), view(bv), view(bv), bias)
    return o.reshape(n, B_OUT), lse.reshape(n, B_OUT)


C_QROWS = 4
C_TQ = C_QROWS * GRID_W


def _na_bias(rpb):
    u = np.arange(C_TQ)[:, None] // GRID_W
    cq = np.arange(C_TQ)[:, None] % GRID_W
    w = np.arange(3 * C_TQ)[None, :] // GRID_W
    ck = np.arange(3 * C_TQ)[None, :] % GRID_W
    cs = np.clip(cq - NA_COLS // 2, 0, GRID_W - NA_COLS)
    col_ok = (ck >= cs) & (ck < cs + NA_COLS)
    row_ok = [(w >= C_QROWS) & (w < C_QROWS + NA_ROWS) & (u >= 0),
              (w >= u) & (w < u + NA_ROWS),
              (w >= 0) & (w < NA_ROWS) & (u >= 0)]
    drow = np.clip((w - C_QROWS) - u + NA_ROWS - 1, 0, 2 * NA_ROWS - 2)
    dcol = np.clip(ck - cq, -(NA_COLS - 1), NA_COLS - 1) + NA_COLS - 1
    drow, dcol = np.broadcast_arrays(drow, dcol)
    vals = rpb[:, jnp.asarray(drow), jnp.asarray(dcol)].astype(F32)
    return jnp.stack([jnp.where(jnp.asarray(r & col_ok)[None], vals, NEG) for r in row_ok])


def _na_kernel(q_ref, kp_ref, kc_ref, kn_ref, vp_ref, vc_ref, vn_ref, bias_ref, o_ref):
    kwin = jnp.concatenate([kp_ref[...], kc_ref[...], kn_ref[...]], axis=0)
    vwin = jnp.concatenate([vp_ref[...], vc_ref[...], vn_ref[...]], axis=0)
    o_tiles, _ = _windowed_heads(q_ref[...], kwin, vwin, bias_ref.at[0], None, False)
    for t in range(2):
        o_ref[:, t * LANES:(t + 1) * LANES] = o_tiles[t]


def mixer_c(c_arr, rpb, batch, seq):
    rows = seq // GRID_W
    assert rows % C_QROWS == 0 and rows >= 3 * C_QROWS
    nblk = rows // C_QROWS
    n = batch * seq
    bias = _na_bias(rpb)

    def spec(col, shift):
        def imap(b, i):
            return (b * nblk + jnp.clip(i + shift, 0, nblk - 1), col)
        return pl.BlockSpec((C_TQ, C_W), imap)

    def bias_map(b, i):
        return (jnp.where(i == 0, 0, jnp.where(i == nblk - 1, 2, 1)), 0, 0, 0)

    return pl.pallas_call(
        _na_kernel,
        grid=(batch, nblk),
        in_specs=[spec(0, 0), spec(1, -1), spec(1, 0), spec(1, 1),
                  spec(2, -1), spec(2, 0), spec(2, 1),
                  pl.BlockSpec((1, C_HEADS, C_TQ, 3 * C_TQ), bias_map)],
        out_specs=pl.BlockSpec((C_TQ, C_W), lambda b, i: (b * nblk + i, 0)),
        out_shape=jax.ShapeDtypeStruct((n, C_W), F32),
        compiler_params=_cparams(("parallel", "parallel")),
        name="mixer_c",
    )(c_arr, c_arr, c_arr, c_arr, c_arr, c_arr, c_arr, bias)


def _rms(x, gain):
    ms = jnp.mean(x * x, axis=-1, keepdims=True)
    return x * lax.rsqrt(ms + EPS) * gain


def _outproj_kernel(oa_ref, ob0_ref, ob1_ref, ob2_ref, l0_ref, l1_ref, l2_ref, oc_ref, x_ref,
                    og_ref, w_ref, nf_ref, wr_ref, x1_ref, hn_ref, aff_ref):
    l0, l1, l2 = l0_ref[...], l1_ref[...], l2_ref[...]
    m = jnp.maximum(jnp.maximum(l0, l1), l2)
    e0, e1, e2 = jnp.exp(l0 - m), jnp.exp(l1 - m), jnp.exp(l2 - m)
    ob = (e0 * ob0_ref[...] + e1 * ob1_ref[...] + e2 * ob2_ref[...]) / (e0 + e1 + e2)
    og = og_ref[...]
    na = _rms(oa_ref[...], og[:, :A_Q]).astype(BF16)
    nb = _rms(ob, og[:, A_Q:A_Q + B_OUT]).astype(BF16)
    nc = _rms(oc_ref[...], og[:, A_Q + B_OUT:]).astype(BF16)
    mix = (jnp.dot(na, w_ref[:A_Q], preferred_element_type=F32)
           + jnp.dot(nb, w_ref[A_Q:A_Q + B_OUT], preferred_element_type=F32)
           + jnp.dot(nc, w_ref[A_Q + B_OUT:], preferred_element_type=F32))
    x1 = x_ref[...] + mix
    x1_ref[...] = x1
    hn = _rms(x1, nf_ref[...])
    hn_ref[...] = hn.astype(BF16)
    logits = jnp.dot(hn, wr_ref[...], preferred_element_type=F32, precision=lax.Precision.HIGHEST)
    p = jnp.exp(logits - jnp.max(logits, axis=-1, keepdims=True))
    aff_ref[...] = p / jnp.sum(p, axis=-1, keepdims=True)


def out_proj(oa, obs, oc, x, out_gain, w_out_bf16, norm_ffn, w_router, tm=512):
    n = x.shape[0]
    row = lambda w: pl.BlockSpec((tm, w), lambda i: (i, 0))
    full = lambda a: pl.BlockSpec(a.shape, lambda i: (0,) * a.ndim)
    og = out_gain.reshape(1, D_MIX)
    nf = norm_ffn.reshape(1, D_MODEL)
    return pl.pallas_call(
        _outproj_kernel,
        grid=(n // tm,),
        in_specs=[row(A_Q)] + [row(B_OUT)] * 6 + [row(C_W), row(D_MODEL),
                  full(og), full(w_out_bf16), full(nf), full(w_router)],
        out_specs=[row(D_MODEL), row(D_MODEL), row(N_EXPERTS)],
        out_shape=[jax.ShapeDtypeStruct((n, D_MODEL), F32),
                   jax.ShapeDtypeStruct((n, D_MODEL), BF16),
                   jax.ShapeDtypeStruct((n, N_EXPERTS), F32)],
        compiler_params=_cparams(("parallel",)),
        name="out_proj",
    )(oa, obs[0][0], obs[1][0], obs[2][0], obs[0][1], obs[1][1], obs[2][1], oc, x,
      og, w_out_bf16, nf, w_router)


ROUTE_BLOCK = LANES


def _route_kernel(aff_ref, lpos_ref, cnt_ref, off_ref, base_ref, base_acc, *, cap):
    e = pl.program_id(0)

    @pl.when(e == 0)
    def _():
        base_acc[...] = jnp.zeros(base_acc.shape, F32)

    bits = lax.bitcast_convert_type(aff_ref[0], jnp.int32)
    nb = bits.shape[0]

    def search(i, thr):
        cand = thr | jnp.left_shift(jnp.int32(1), 30 - i)
        cnt = jnp.sum((bits >= cand).astype(F32), keepdims=True)
        return jnp.where(cnt >= cap, cand, thr)

    thr = lax.fori_loop(0, 31, search, jnp.zeros((1, 1), jnp.int32))
    gt = bits > thr
    eq = bits == thr
    need = cap - jnp.sum(gt.astype(F32), keepdims=True)

    ri = lax.broadcasted_iota(jnp.int32, (LANES, LANES), 0)
    ci = lax.broadcasted_iota(jnp.int32, (LANES, LANES), 1)
    upper = (ri <= ci).astype(F32).astype(BF16)
    ones = jnp.ones((LANES, LANES), BF16)
    rr = lax.broadcasted_iota(jnp.int32, (nb, nb), 0)
    rc = lax.broadcasted_iota(jnp.int32, (nb, nb), 1)
    strict_lower = (rc < rr).astype(F32).astype(BF16)

    def prefix(mask_f32):
        mb = mask_f32.astype(BF16)
        in_row = jnp.dot(mb, upper, preferred_element_type=F32)
        tot = jnp.dot(mb, ones, preferred_element_type=F32)
        offs = jnp.dot(strict_lower, tot.astype(BF16), preferred_element_type=F32)
        return in_row, tot, offs

    eq_f = eq.astype(F32)
    in_row_e, _, offs_e = prefix(eq_f)
    rank_eq = offs_e + in_row_e - eq_f
    sel = gt | (eq & (rank_eq < need))
    in_row, tot, offs = prefix(sel.astype(F32))
    base = base_acc[...]
    lpos_ref[0] = jnp.where(sel, base + in_row - 1.0, -1.0).astype(jnp.int32)
    cnt_ref[0] = tot.astype(jnp.int32)
    off_ref[0] = offs.astype(jnp.int32)
    base_ref[0] = base.astype(jnp.int32)
    base_acc[...] = base + tot


def route(aff, cap):
    n = aff.shape[0]
    nb = n // LANES
    aff_t = aff.T.reshape(N_EXPERTS, nb, LANES)
    spec = pl.BlockSpec((1, nb, LANES), lambda e: (e, 0, 0))
    outs = pl.pallas_call(
        functools.partial(_route_kernel, cap=cap),
        grid=(N_EXPERTS,),
        in_specs=[spec],
        out_specs=[spec] * 4,
        out_shape=[jax.ShapeDtypeStruct((N_EXPERTS, nb, LANES), jnp.int32)] * 4,
        scratch_shapes=[pltpu.VMEM((nb, LANES), F32)],
        compiler_params=_cparams(("arbitrary",)),
        name="route",
    )(aff_t)
    lpos, cnt, off, base = outs
    scal = lambda a: a[:, :, 0].T.reshape(-1)
    lpos_t = lpos.transpose(1, 0, 2)
    lpos_n = lpos.reshape(N_EXPERTS, n).T
    return lpos_t, lpos_n, scal(cnt), scal(off), scal(base)


ROW_TILE = (D_MODEL // LANES, LANES)
SEG_BITS = tuple(1 << k for k in range(7, -1, -1))
MAX_BLOCK_ROWS = N_EXPERTS * ROUTE_BLOCK


def _segment_copies(fn, cnt_s, b, make):
    for e in range(N_EXPERTS):
        c = cnt_s[b * N_EXPERTS + e]
        for bit in SEG_BITS:
            done = (c // (2 * bit)) * (2 * bit)

            @pl.when((c & bit) != 0)
            def _():
                fn(make(e, done, bit))


def _dispatch_kernel(cnt_s, off_s, base_s, h_ref, lpos_ref, xe_hbm, xc_ref, sem, *, cap):
    b = pl.program_id(0)
    last = b * N_EXPERTS + N_EXPERTS - 1
    ctot = base_s[last] + cnt_s[last]
    h = h_ref[...]
    lp = lpos_ref[0]

    def chunk(k, carry):
        r0 = pl.multiple_of(k * LANES, LANES)
        rho = r0 + lax.broadcasted_iota(jnp.int32, (LANES, ROUTE_BLOCK), 0)
        hit = lp[0:1, :] == rho
        for e in range(1, N_EXPERTS):
            hit = hit | (lp[e:e + 1, :] == rho)
        onehot = hit.astype(F32).astype(BF16)
        for f in range(ROW_TILE[0]):
            xc_ref[pl.ds(r0, LANES), f, :] = jnp.dot(
                onehot, h[:, f * LANES:(f + 1) * LANES], preferred_element_type=F32)
        return carry

    lax.fori_loop(0, (ctot + LANES - 1) // LANES, chunk, 0)

    def make(e, done, bit):
        src = base_s[b * N_EXPERTS + e] + done
        dst = e * cap + off_s[b * N_EXPERTS + e] + done
        return pltpu.make_async_copy(xc_ref.at[pl.ds(src, bit)], xe_hbm.at[pl.ds(dst, bit)], sem)

    _segment_copies(lambda cp: cp.start(), cnt_s, b, make)
    _segment_copies(lambda cp: cp.wait(), cnt_s, b, make)


def dispatch(hn, lpos_t, cnt, off, base, cap):
    n = hn.shape[0]
    nb = n // ROUTE_BLOCK
    grid_spec = pltpu.PrefetchScalarGridSpec(
        num_scalar_prefetch=3,
        grid=(nb,),
        in_specs=[pl.BlockSpec((ROUTE_BLOCK, D_MODEL), lambda b, *_: (b, 0)),
                  pl.BlockSpec((1, N_EXPERTS, ROUTE_BLOCK), lambda b, *_: (b, 0, 0))],
        out_specs=pl.BlockSpec(memory_space=pl.ANY),
        scratch_shapes=[pltpu.VMEM((MAX_BLOCK_ROWS,) + ROW_TILE, F32),
                        pltpu.SemaphoreType.DMA(())],
    )
    return pl.pallas_call(
        functools.partial(_dispatch_kernel, cap=cap),
        grid_spec=grid_spec,
        out_shape=jax.ShapeDtypeStruct((N_EXPERTS * cap,) + ROW_TILE, F32),
        compiler_params=_cparams(("arbitrary",)),
        name="dispatch",
    )(cnt, off, base, hn, lpos_t)


def _ffn_kernel(xe_ref, wg_ref, wu_ref, wd_ref, ye_ref, xs_ref, *, hid_chunk):
    m = xs_ref.shape[0]
    nf = ROW_TILE[0]
    for f in range(nf):
        xs_ref[:, f * LANES:(f + 1) * LANES] = xe_ref[pl.ds(f, m, stride=nf), :].astype(BF16)
    x = xs_ref[...]
    acc = jnp.zeros((m, D_MODEL), F32)
    for c in range(D_EXPERT // hid_chunk):
        cols = slice(c * hid_chunk, (c + 1) * hid_chunk)
        g = jnp.dot(x, wg_ref[0, :, cols], preferred_element_type=F32)
        u = jnp.dot(x, wu_ref[0, :, cols], preferred_element_type=F32)
        hid = (g * jax.nn.sigmoid(g) * u).astype(BF16)
        acc = acc + jnp.dot(hid, wd_ref[0, cols, :], preferred_element_type=F32)
    for f in range(nf):
        ye_ref[pl.ds(f, m, stride=nf), :] = acc[:, f * LANES:(f + 1) * LANES]


def expert_ffn(xe, wg, wu, wd, cap, tm=512, hid_chunk=512):
    nf = ROW_TILE[0]
    rows = xe.shape[0]
    tiles = cap // tm
    xe2 = xe.reshape(rows * nf, LANES)
    io_spec = pl.BlockSpec((tm * nf, LANES), lambda e, j: (e * tiles + j, 0))
    ye = pl.pallas_call(
        functools.partial(_ffn_kernel, hid_chunk=hid_chunk),
        grid=(N_EXPERTS, tiles),
        in_specs=[io_spec,
                  pl.BlockSpec((1, D_MODEL, D_EXPERT), lambda e, j: (e, 0, 0)),
                  pl.BlockSpec((1, D_MODEL, D_EXPERT), lambda e, j: (e, 0, 0)),
                  pl.BlockSpec((1, D_EXPERT, D_MODEL), lambda e, j: (e, 0, 0))],
        out_specs=io_spec,
        out_shape=jax.ShapeDtypeStruct((rows * nf, LANES), F32),
        scratch_shapes=[pltpu.VMEM((tm, D_MODEL), BF16)],
        compiler_params=_cparams(("parallel", "parallel")),
        name="expert_ffn",
    )(xe2, wg, wu, wd)
    return ye.reshape((rows,) + ROW_TILE)


def _combine_kernel(cnt_s, off_s, base_s, x_ref, lpos_ref, aff_ref, fg_ref, ye_hbm, o_ref,
                    yc_ref, sem, *, cap, final):
    b = pl.program_id(0)

    @pl.when(b == 0)
    def _():
        yc_ref[...] = jnp.zeros(yc_ref.shape, F32)

    def make(e, done, bit):
        src = e * cap + off_s[b * N_EXPERTS + e] + done
        dst = base_s[b * N_EXPERTS + e] + done
        return pltpu.make_async_copy(ye_hbm.at[pl.ds(src, bit)], yc_ref.at[pl.ds(dst, bit)], sem)

    _segment_copies(lambda cp: cp.start(), cnt_s, b, make)
    _segment_copies(lambda cp: cp.wait(), cnt_s, b, make)

    last = b * N_EXPERTS + N_EXPERTS - 1
    ctot = base_s[last] + cnt_s[last]
    lp = lpos_ref[...]
    af = aff_ref[...]
    o_ref[...] = x_ref[...]

    def chunk(k, carry):
        r0 = pl.multiple_of(k * LANES, LANES)
        rho = r0 + lax.broadcasted_iota(jnp.int32, (ROUTE_BLOCK, LANES), 1)
        w = jnp.zeros((ROUTE_BLOCK, LANES), F32)
        for e in range(N_EXPERTS):
            w = jnp.where(lp[:, e:e + 1] == rho, af[:, e:e + 1], w)
        hi = w.astype(BF16)
        lo = (w - hi.astype(F32)).astype(BF16)
        for f in range(ROW_TILE[0]):
            y = yc_ref[pl.ds(r0, LANES), f, :].astype(BF16)
            o_ref[:, f * LANES:(f + 1) * LANES] += (
                jnp.dot(hi, y, preferred_element_type=F32)
                + jnp.dot(lo, y, preferred_element_type=F32))
        return carry

    lax.fori_loop(0, (ctot + LANES - 1) // LANES, chunk, 0)
    if final:
        o_ref[...] = _rms(o_ref[...], fg_ref[...])


def combine(x1, lpos_n, aff, final_gain, ye, cnt, off, base, cap, final):
    n = x1.shape[0]
    nb = n // ROUTE_BLOCK
    grid_spec = pltpu.PrefetchScalarGridSpec(
        num_scalar_prefetch=3,
        grid=(nb,),
        in_specs=[pl.BlockSpec((ROUTE_BLOCK, D_MODEL), lambda b, *_: (b, 0)),
                  pl.BlockSpec((ROUTE_BLOCK, N_EXPERTS), lambda b, *_: (b, 0)),
                  pl.BlockSpec((ROUTE_BLOCK, N_EXPERTS), lambda b, *_: (b, 0)),
                  pl.BlockSpec((1, D_MODEL), lambda b, *_: (0, 0)),
                  pl.BlockSpec(memory_space=pl.ANY)],
        out_specs=pl.BlockSpec((ROUTE_BLOCK, D_MODEL), lambda b, *_: (b, 0)),
        scratch_shapes=[pltpu.VMEM((MAX_BLOCK_ROWS,) + ROW_TILE, F32),
                        pltpu.SemaphoreType.DMA(())],
    )
    return pl.pallas_call(
        functools.partial(_combine_kernel, cap=cap, final=final),
        grid_spec=grid_spec,
        out_shape=jax.ShapeDtypeStruct((n, D_MODEL), F32),
        compiler_params=_cparams(("arbitrary",)),
        name="combine",
    )(cnt, off, base, x1, lpos_n, aff, final_gain.reshape(1, D_MODEL), ye)


def moe(x1, hn, aff, wg, wu, wd, final_gain, final):
    n = x1.shape[0]
    cap = EC_CAPACITY * n // N_EXPERTS
    lpos_t, lpos_n, cnt, off, base = route(aff, cap)
    xe = dispatch(hn, lpos_t, cnt, off, base, cap)
    ye = expert_ffn(xe, wg, wu, wd, cap)
    return combine(x1, lpos_n, aff, final_gain, ye, cnt, off, base, cap, final)


def _trunk(x, p):
    batch, seq, _ = x.shape
    n = batch * seq
    x = x.reshape(n, D_MODEL)
    for l in range(DEPTH):
        a_arr, bq, bk, bv, c_arr = in_proj(x, p["norm_mix"][l], p["w_in"][l])
        qx, k2, vx = mixer_a_prep(a_arr, p["q_gain"][l], p["k_gain"][l], batch, seq)
        oa = mixer_a(qx, k2, vx, batch, seq)
        obs = [mixer_b_group(bq, bk, bv, p["rel_bias"], g, batch, seq) for g in range(B_GROUPS)]
        oc = mixer_c(c_arr, p["na_rpb"][l], batch, seq)
        x1, hn, aff = out_proj(oa, obs, oc, x, p["out_gain"][l], p["w_out"][l],
                               p["norm_ffn"][l], p["w_router"][l])
        x = moe(x1, hn, aff, p["w_gate"][l], p["w_up"][l], p["w_down"][l],
                p["final_norm"], final=(l == DEPTH - 1))
    return x.reshape(batch, seq, D_MODEL)


def kernel(x_prompt, x_sample, w_in, w_out, norm_mix, norm_ffn, q_gain, k_gain, out_gain,
           na_rpb, rel_bias, w_router, w_gate, w_up, w_down, final_norm):
    p = dict(w_in=w_in.astype(BF16), w_out=w_out.astype(BF16), norm_mix=norm_mix,
             norm_ffn=norm_ffn, q_gain=q_gain, k_gain=k_gain, out_gain=out_gain,
             na_rpb=na_rpb, rel_bias=rel_bias, w_router=w_router,
             w_gate=w_gate.astype(BF16), w_up=w_up.astype(BF16), w_down=w_down.astype(BF16),
             final_norm=final_norm)
    return (_trunk(x_prompt, p), _trunk(x_sample, p))
```

```python
import functools
import math

import jax
import jax.numpy as jnp
import numpy as np
from jax import lax
from jax.experimental import pallas as pl
from jax.experimental.pallas import tpu as pltpu

D_MODEL = 1024
DEPTH = 2
GRID_W = 64
HEAD_DIM = 64
EPS = 1e-6
A_HEADS = 8
A_KV_HEADS = 2
A_GROUP = A_HEADS // A_KV_HEADS
ROPE_THETA = 10000.0
DILATED_PAIRS = ((128, 1), (512, 4), (2048, 16))
B_GROUPS = 3
B_HEADS = 4
C_HEADS = 4
NA_ROWS = 8
NA_COLS = 16
N_BUCKETS = 32
MAX_DISTANCE = 2048
N_EXPERTS = 16
EC_CAPACITY = 2
D_EXPERT = 2 * D_MODEL
A_Q = A_HEADS * HEAD_DIM
A_KV = A_KV_HEADS * HEAD_DIM
B_OUT = B_HEADS * HEAD_DIM
C_W = C_HEADS * HEAD_DIM
D_IN = 3840
D_MIX = A_Q + B_OUT + C_W

LANES = 128
PROJ_CHUNK = 768
NEG = -1e30
LOG2E = 1.4426950408889634
VMEM_LIMIT = 56 * 1024 * 1024

F32 = jnp.float32
BF16 = jnp.bfloat16


def _cparams(sem):
    return pltpu.CompilerParams(dimension_semantics=sem, vmem_limit_bytes=VMEM_LIMIT)


def permute_w_in(w_in):
    b0 = A_Q + 2 * A_KV
    bw = B_GROUPS * B_OUT
    parts = [w_in[..., :b0], w_in[..., b0 + 3 * bw:]]
    for g in range(B_GROUPS):
        parts += [w_in[..., b0 + s * bw + g * B_OUT:b0 + s * bw + (g + 1) * B_OUT] for s in range(3)]
    return jnp.concatenate(parts, axis=-1)


def _inproj_kernel(x_ref, g_ref, w_ref, a_ref, c_ref, *rest):
    g_refs, scr_ref = rest[:B_GROUPS], rest[B_GROUPS]
    x = x_ref[...]
    tm = x.shape[0]
    ms = jnp.mean(x * x, axis=-1, keepdims=True)
    h = (x * lax.rsqrt(ms + EPS) * g_ref[...]).astype(BF16)

    def proj(k):
        return jnp.dot(h, w_ref[:, k * PROJ_CHUNK:(k + 1) * PROJ_CHUNK], preferred_element_type=F32)

    a_ref[...] = proj(0).astype(BF16)
    c_ref[...] = proj(1).astype(BF16)
    for g, g_ref_out in enumerate(g_refs):
        d = DILATED_PAIRS[g][1]
        if d == 1:
            g_ref_out[0, 0] = proj(2 + g).astype(BF16)
        else:
            res = proj(2 + g)
            ntile = PROJ_CHUNK // LANES
            for c in range(ntile):
                scr_ref[c] = res[:, c * LANES:(c + 1) * LANES]
            for r in range(d):
                for c in range(ntile):
                    g_ref_out[0, r, :, c * LANES:(c + 1) * LANES] = (
                        scr_ref[c, pl.ds(r, tm // d, stride=d), :].astype(BF16))


def in_proj(x, gain, w_perm_bf16, batch, seq, tm=512):
    n = batch * seq
    nt = seq // tm
    row = pl.BlockSpec((tm, PROJ_CHUNK), lambda i: (i, 0))
    g_specs, g_shapes = [], []
    for _, d in DILATED_PAIRS:
        g_specs.append(pl.BlockSpec((1, d, tm // d, PROJ_CHUNK), lambda i: (i // nt, 0, i % nt, 0)))
        g_shapes.append(jax.ShapeDtypeStruct((batch, d, seq // d, PROJ_CHUNK), BF16))
    return pl.pallas_call(
        _inproj_kernel,
        grid=(n // tm,),
        in_specs=[pl.BlockSpec((tm, D_MODEL), lambda i: (i, 0)),
                  pl.BlockSpec((1, D_MODEL), lambda i: (0, 0)),
                  pl.BlockSpec((D_MODEL, D_IN), lambda i: (0, 0))],
        out_specs=[row, row] + g_specs,
        out_shape=[jax.ShapeDtypeStruct((n, PROJ_CHUNK), BF16)] * 2 + g_shapes,
        scratch_shapes=[pltpu.VMEM((PROJ_CHUNK // LANES, tm, LANES), F32)],
        compiler_params=_cparams(("arbitrary",)),
        name="in_proj",
    )(x, gain.reshape(1, D_MODEL), w_perm_bf16)


def _rope_tables(seq):
    t = np.arange(seq)
    row = (t // GRID_W).astype(np.float64)
    col = (t % GRID_W).astype(np.float64)
    half = HEAD_DIM // 2
    freqs = ROPE_THETA ** (-np.arange(0, half, 2, dtype=np.float64) / half)
    ang = np.concatenate([row[:, None] * freqs, col[:, None] * freqs], axis=-1)
    cos = np.repeat(np.cos(ang), 2, axis=-1)
    sin = np.repeat(np.sin(ang), 2, axis=-1)
    sign = np.tile(np.array([-1.0, 1.0], np.float32), HEAD_DIM // 2)
    cos2 = np.tile(cos, (1, 2)).astype(np.float32)
    sin2 = np.tile(sin * sign, (1, 2)).astype(np.float32)
    return jnp.asarray(cos2), jnp.asarray(sin2)


def _headnorm_rope(x, gain2, cos, sin, seg):
    sq = x * x
    hi = sq.astype(BF16)
    lo = (sq - hi.astype(F32)).astype(BF16)
    ms = (jnp.dot(hi, seg, preferred_element_type=F32)
          + jnp.dot(lo, seg, preferred_element_type=F32))
    y = x * lax.rsqrt(ms + EPS) * gain2
    lane = lax.broadcasted_iota(jnp.int32, y.shape, 1)
    even = (lane % 2) == 0
    swapped = jnp.where(even, pltpu.roll(y, LANES - 1, 1), pltpu.roll(y, 1, 1))
    return y * cos + swapped * sin


def _aprep_kernel(a_ref, cos_ref, sin_ref, qg_ref, kg_ref, q_ref, k_ref, v_ref):
    cos = cos_ref[...]
    sin = sin_ref[...]
    r = lax.broadcasted_iota(jnp.int32, (LANES, LANES), 0) // HEAD_DIM
    c = lax.broadcasted_iota(jnp.int32, (LANES, LANES), 1) // HEAD_DIM
    seg = jnp.where(r == c, 1.0 / HEAD_DIM, 0.0).astype(BF16)
    tm = a_ref.shape[0]
    lane = lax.broadcasted_iota(jnp.int32, (tm, LANES), 1)
    low = lane < HEAD_DIM
    qscale = (HEAD_DIM ** -0.5) * LOG2E
    for t in range(A_Q // LANES):
        x = a_ref[:, t * LANES:(t + 1) * LANES].astype(F32)
        y = _headnorm_rope(x, qg_ref[...], cos, sin, seg) * qscale
        swapped = pltpu.roll(y, HEAD_DIM, 1)
        if (2 * t) // A_GROUP == 0:
            q_even, q_odd = jnp.where(low, y, 0.0), jnp.where(low, swapped, 0.0)
        else:
            q_even, q_odd = jnp.where(low, 0.0, swapped), jnp.where(low, 0.0, y)
        q_ref[0, 2 * t] = q_even.astype(BF16)
        q_ref[0, 2 * t + 1] = q_odd.astype(BF16)
    xk = a_ref[:, A_Q:A_Q + LANES].astype(F32)
    k_ref[0] = _headnorm_rope(xk, kg_ref[...], cos, sin, seg).astype(BF16)
    xv = a_ref[:, A_Q + LANES:A_Q + 2 * LANES].astype(F32)
    one_col = (lane == HEAD_DIM).astype(F32)
    v_ref[0, 0] = (jnp.where(low, xv, 0.0) + one_col).astype(BF16)
    v_ref[0, 1] = (jnp.where(low, pltpu.roll(xv, HEAD_DIM, 1), 0.0) + one_col).astype(BF16)


def mixer_a_prep(a_arr, q_gain, k_gain, batch, seq, tm=512):
    cos2, sin2 = _rope_tables(seq)
    nb = seq // tm
    qg = jnp.tile(q_gain.astype(F32), 2).reshape(1, LANES)
    kg = jnp.tile(k_gain.astype(F32), 2).reshape(1, LANES)
    return pl.pallas_call(
        _aprep_kernel,
        grid=(batch, nb),
        in_specs=[pl.BlockSpec((tm, PROJ_CHUNK), lambda b, i: (b * nb + i, 0)),
                  pl.BlockSpec((tm, LANES), lambda b, i: (i, 0)),
                  pl.BlockSpec((tm, LANES), lambda b, i: (i, 0)),
                  pl.BlockSpec((1, LANES), lambda b, i: (0, 0)),
                  pl.BlockSpec((1, LANES), lambda b, i: (0, 0))],
        out_specs=[pl.BlockSpec((1, A_HEADS, tm, LANES), lambda b, i: (b, 0, i, 0)),
                   pl.BlockSpec((1, tm, LANES), lambda b, i: (b, i, 0)),
                   pl.BlockSpec((1, A_KV_HEADS, tm, LANES), lambda b, i: (b, 0, i, 0))],
        out_shape=[jax.ShapeDtypeStruct((batch, A_HEADS, seq, LANES), BF16),
                   jax.ShapeDtypeStruct((batch, seq, LANES), BF16),
                   jax.ShapeDtypeStruct((batch, A_KV_HEADS, seq, LANES), BF16)],
        compiler_params=_cparams(("arbitrary", "arbitrary")),
        name="mixer_a_prep",
    )(a_arr, cos2, sin2, qg, kg)


FLASH_ROWS = 256


def _flash_kernel(q_ref, k_ref, v_ref, o_ref, m_ref, acc_ref):
    j = pl.program_id(3)
    tq = q_ref.shape[2]
    tk = k_ref.shape[1]

    @pl.when(j == 0)
    def _():
        m_ref[...] = jnp.full(m_ref.shape, -jnp.inf, F32)
        acc_ref[...] = jnp.zeros(acc_ref.shape, F32)

    k = k_ref[0]
    v = v_ref[0, 0]
    for h in range(A_GROUP):
        for r0 in range(0, tq, FLASH_ROWS):
            rows = pl.ds(r0, FLASH_ROWS)
            s = lax.dot_general(q_ref[0, h, rows, :], k, (((1,), (1,)), ((), ())),
                                preferred_element_type=F32)
            m_prev = m_ref[h, rows, :]
            m_new = jnp.maximum(m_prev, jnp.max(s, axis=-1, keepdims=True))
            alpha = jnp.exp2(m_prev - m_new)
            p = jnp.exp2(s - jnp.tile(m_new, (1, tk // LANES))).astype(BF16)
            acc_ref[h, rows, :] = alpha * acc_ref[h, rows, :] + jnp.dot(
                p, v, preferred_element_type=F32)
            m_ref[h, rows, :] = m_new

    @pl.when(j == pl.num_programs(3) - 1)
    def _():
        lane = lax.broadcasted_iota(jnp.int32, (tq, LANES), 1)
        low = lane < HEAD_DIM

        def normed(h):
            acc = acc_ref[h]
            return acc / acc[:, HEAD_DIM:HEAD_DIM + 1]

        for t in range(A_GROUP // 2):
            o_ref[:, t * LANES:(t + 1) * LANES] = jnp.where(
                low, normed(2 * t), pltpu.roll(normed(2 * t + 1), HEAD_DIM, 1)).astype(o_ref.dtype)


def mixer_a(qx, k2, vx, batch, seq, tq=512, tk=2048):
    nq = seq // tq
    grp_w = A_GROUP * HEAD_DIM
    return pl.pallas_call(
        _flash_kernel,
        grid=(batch, A_KV_HEADS, nq, seq // tk),
        in_specs=[pl.BlockSpec((1, A_GROUP, tq, LANES), lambda b, h, i, j: (b, h, i, 0)),
                  pl.BlockSpec((1, tk, LANES), lambda b, h, i, j: (b, j, 0)),
                  pl.BlockSpec((1, 1, tk, LANES), lambda b, h, i, j: (b, h, j, 0))],
        out_specs=pl.BlockSpec((tq, grp_w), lambda b, h, i, j: (b * nq + i, h)),
        out_shape=jax.ShapeDtypeStruct((batch * seq, A_Q), F32),
        scratch_shapes=[pltpu.VMEM((A_GROUP, tq, LANES), F32),
                        pltpu.VMEM((A_GROUP, tq, LANES), F32)],
        compiler_params=_cparams(("arbitrary",) * 4),
        name="mixer_a_flash",
    )(qx, k2, vx)


B_TL = 128
B_CHAINS = 4


def _toeplitz(vec, nrow, ncol, shift):
    period = vec.shape[-1] + 1
    lead = vec.shape[:-1]
    padded = jnp.concatenate([vec, jnp.zeros(lead + (1,), vec.dtype)], axis=-1)
    flat = jnp.tile(padded, (1,) * len(lead) + (nrow,))[..., :nrow * (period - 1)]
    skew = flat.reshape(lead + (nrow, period - 1))
    return skew[..., shift:shift + ncol]
def _t5_bucket(rel):
    nb = N_BUCKETS // 2
    max_exact = nb // 2
    ret = (rel > 0).astype(jnp.int32) * nb
    n = jnp.abs(rel)
    large = max_exact + (jnp.log(jnp.maximum(n, 1).astype(jnp.float32) / max_exact)
                         / math.log(MAX_DISTANCE / max_exact) * (nb - max_exact)).astype(jnp.int32)
    large = jnp.minimum(large, nb - 1)
    return ret + jnp.where(n < max_exact, n, large)


def _dilated_bias(rel_bias_g, dilation, tl, radius):
    off = np.arange(2 * tl + 2 * radius - 1) - (tl - 1) - radius
    per_off = rel_bias_g[_t5_bucket(jnp.asarray(off * dilation))].astype(F32).T
    per_off = jnp.where(jnp.asarray(np.abs(off) <= radius)[None], per_off, NEG)
    return _toeplitz(per_off, tl, tl + 2 * radius, tl - 1)


def _head_mask(shape, h):
    lane = lax.broadcasted_iota(jnp.int32, shape, 1)
    return (lane < HEAD_DIM) if h % 2 == 0 else (lane >= HEAD_DIM)


def _windowed_heads(q, kwin, vwin, bias_ref, valid, want_lse):
    tl = q.shape[0]
    scale = jnp.asarray(HEAD_DIM ** -0.5, q.dtype)
    o_tiles, lse_tiles = [], []
    for t in range(2):
        qt = q[:, t * LANES:(t + 1) * LANES] * scale
        kt = kwin[:, t * LANES:(t + 1) * LANES]
        vt = vwin[:, t * LANES:(t + 1) * LANES]
        o_pair, lse_pair = [], []
        for hh in range(2):
            h = 2 * t + hh
            sel = _head_mask((tl, LANES), h)
            qm = jnp.where(sel, qt, jnp.zeros_like(qt))
            s = lax.dot_general(qm, kt, (((1,), (1,)), ((), ())), preferred_element_type=F32)
            s = s + bias_ref[h]
            if valid is not None:
                s = jnp.where(valid, s, NEG)
            m = jnp.max(s, axis=-1, keepdims=True)
            p = jnp.exp(s - m)
            den = jnp.sum(p, axis=-1, keepdims=True)
            pv = jnp.dot(p.astype(BF16), vt, preferred_element_type=F32)
            o_pair.append(pv / den)
            lse_pair.append(m + jnp.log(den))
        sel0 = _head_mask((tl, LANES), 0)
        o_tiles.append(jnp.where(sel0, o_pair[0], o_pair[1]))
        if want_lse:
            lse_tiles.append(jnp.where(sel0, lse_pair[0], lse_pair[1]))
    return o_tiles, lse_tiles


def _dilated_kernel(q_ref, kp_ref, km_ref, kn_ref, vp_ref, vm_ref, vn_ref, bias_ref,
                    o_ref, lse_ref, *, sub_len, radius):
    i = pl.program_id(2)
    nres, tls = q_ref.shape[0], q_ref.shape[1]
    nsub = tls // B_TL
    win = B_TL + 2 * radius
    for r in range(nres):
        kwin = jnp.concatenate([kp_ref[r], km_ref[r], kn_ref[r]], axis=0)
        vwin = jnp.concatenate([vp_ref[r], vm_ref[r], vn_ref[r]], axis=0)
        for j in range(nsub):
            valid = None
            if j == 0 or j == nsub - 1:
                kpos = (i * tls + j * B_TL - radius
                        + lax.broadcasted_iota(jnp.int32, (1, win), 1))
                valid = (kpos >= 0) & (kpos < sub_len)
            rows = slice(j * B_TL, (j + 1) * B_TL)
            o_tiles, lse_tiles = _windowed_heads(
                q_ref[r, rows, :], kwin[j * B_TL:j * B_TL + win], vwin[j * B_TL:j * B_TL + win],
                bias_ref, valid, True)
            for t in range(2):
                o_ref[r, rows, t * LANES:(t + 1) * LANES] = o_tiles[t]
                lse_ref[r, rows, t * LANES:(t + 1) * LANES] = lse_tiles[t]


def mixer_b_group(gqkv, rel_bias, g):
    batch, d, sub_len, _ = gqkv.shape
    window = DILATED_PAIRS[g][0]
    radius = window // (2 * d)
    assert sub_len % B_TL == 0 and B_TL % radius == 0
    nsub = min(B_CHAINS, sub_len // B_TL)
    nres = min(d, B_CHAINS // nsub)
    tls = nsub * B_TL
    hpb = tls // radius
    nhalo = sub_len // radius
    bias = _dilated_bias(rel_bias[:, g * B_HEADS:(g + 1) * B_HEADS], d, B_TL, radius)

    def main(col):
        return pl.BlockSpec((None, nres, tls, B_OUT), lambda b, r, i: (b, r, i, col))

    def prev(col):
        return pl.BlockSpec((None, nres, radius, B_OUT),
                            lambda b, r, i: (b, r, jnp.maximum(i * hpb - 1, 0), col))

    def nxt(col):
        return pl.BlockSpec((None, nres, radius, B_OUT),
                            lambda b, r, i: (b, r, jnp.minimum((i + 1) * hpb, nhalo - 1), col))

    return pl.pallas_call(
        functools.partial(_dilated_kernel, sub_len=sub_len, radius=radius),
        grid=(batch, d // nres, sub_len // tls),
        in_specs=[main(0), prev(1), main(1), nxt(1), prev(2), main(2), nxt(2),
                  pl.BlockSpec((B_HEADS, B_TL, B_TL + 2 * radius), lambda b, r, i: (0, 0, 0))],
        out_specs=[main(0), main(0)],
        out_shape=[jax.ShapeDtypeStruct((batch, d, sub_len, B_OUT), F32)] * 2,
        compiler_params=_cparams(("arbitrary",) * 3),
        name=f"mixer_b_g{g}",
    )(gqkv, gqkv, gqkv, gqkv, gqkv, gqkv, gqkv, bias)


C_QROWS = 4
C_TQ = C_QROWS * GRID_W


def _na_bias(rpb):
    u = np.arange(C_TQ)[:, None] // GRID_W
    cq = np.arange(C_TQ)[:, None] % GRID_W
    w = np.arange(3 * C_TQ)[None, :] // GRID_W
    ck = np.arange(3 * C_TQ)[None, :] % GRID_W
    cs = np.clip(cq - NA_COLS // 2, 0, GRID_W - NA_COLS)
    col_ok = (ck >= cs) & (ck < cs + NA_COLS)
    row_ok = [(w >= C_QROWS) & (w < C_QROWS + NA_ROWS) & (u >= 0),
              (w >= u) & (w < u + NA_ROWS),
              (w >= 0) & (w < NA_ROWS) & (u >= 0)]
    side = GRID_W - NA_COLS
    padded = jnp.pad(rpb.astype(F32), ((0, 0), (0, 0), (side, side)))
    blocks = _toeplitz(padded, GRID_W, GRID_W, GRID_W - 1)
    drow = np.clip((np.arange(3 * C_QROWS)[None, :] - C_QROWS) - np.arange(C_QROWS)[:, None]
                   + NA_ROWS - 1, 0, 2 * NA_ROWS - 2)
    tiles = jnp.take(blocks, jnp.asarray(drow.reshape(-1)), axis=1)
    tiles = tiles.reshape(C_HEADS, C_QROWS, 3 * C_QROWS, GRID_W, GRID_W)
    vals = tiles.transpose(0, 1, 3, 2, 4).reshape(C_HEADS, C_TQ, 3 * C_TQ)
    return jnp.stack([jnp.where(jnp.asarray(r & col_ok)[None], vals, NEG) for r in row_ok])


def _na_kernel(q_ref, kp_ref, kc_ref, kn_ref, vp_ref, vc_ref, vn_ref, bias_ref, o_ref):
    kwin = jnp.concatenate([kp_ref[...], kc_ref[...], kn_ref[...]], axis=0)
    vwin = jnp.concatenate([vp_ref[...], vc_ref[...], vn_ref[...]], axis=0)
    o_tiles, _ = _windowed_heads(q_ref[...], kwin, vwin, bias_ref.at[0], None, False)
    for t in range(2):
        o_ref[:, t * LANES:(t + 1) * LANES] = o_tiles[t]


def mixer_c(c_arr, rpb, batch, seq):
    rows = seq // GRID_W
    assert rows % C_QROWS == 0 and rows >= 3 * C_QROWS
    nblk = rows // C_QROWS
    n = batch * seq
    bias = _na_bias(rpb)

    def spec(col, shift):
        def imap(b, i):
            return (b * nblk + jnp.clip(i + shift, 0, nblk - 1), col)
        return pl.BlockSpec((C_TQ, C_W), imap)

    def bias_map(b, i):
        return (jnp.where(i == 0, 0, jnp.where(i == nblk - 1, 2, 1)), 0, 0, 0)

    return pl.pallas_call(
        _na_kernel,
        grid=(batch, nblk),
        in_specs=[spec(0, 0), spec(1, -1), spec(1, 0), spec(1, 1),
                  spec(2, -1), spec(2, 0), spec(2, 1),
                  pl.BlockSpec((1, C_HEADS, C_TQ, 3 * C_TQ), bias_map)],
        out_specs=pl.BlockSpec((C_TQ, C_W), lambda b, i: (b * nblk + i, 0)),
        out_shape=jax.ShapeDtypeStruct((n, C_W), F32),
        compiler_params=_cparams(("arbitrary", "arbitrary")),
        name="mixer_c",
    )(c_arr, c_arr, c_arr, c_arr, c_arr, c_arr, c_arr, bias)


def _rms(x, gain):
    ms = jnp.mean(x * x, axis=-1, keepdims=True)
    return x * lax.rsqrt(ms + EPS) * gain


def _token_order(src_ref, scr_ref):
    d, sub = src_ref.shape[1], src_ref.shape[2]
    if d == 1:
        return src_ref[0, 0]
    ntile = scr_ref.shape[0]
    for r in range(d):
        for c in range(ntile):
            scr_ref[c, pl.ds(r, sub, stride=d), :] = src_ref[0, r, :, c * LANES:(c + 1) * LANES]
    return jnp.concatenate([scr_ref[c] for c in range(ntile)], axis=1)


def _outproj_kernel(oa_ref, ob0_ref, ob1_ref, ob2_ref, l0_ref, l1_ref, l2_ref, oc_ref, x_ref,
                    og_ref, w_ref, nf_ref, wr_ref, x1_ref, hn_ref, aff_ref, *scr):
    ob0, ob1, ob2 = (_token_order(r, s) for r, s in zip((ob0_ref, ob1_ref, ob2_ref), scr[:3]))
    l0, l1, l2 = (_token_order(r, s) for r, s in zip((l0_ref, l1_ref, l2_ref), scr[3:]))
    m = jnp.maximum(jnp.maximum(l0, l1), l2)
    e0, e1, e2 = jnp.exp(l0 - m), jnp.exp(l1 - m), jnp.exp(l2 - m)
    ob = (e0 * ob0 + e1 * ob1 + e2 * ob2) / (e0 + e1 + e2)
    og = og_ref[...]
    na = _rms(oa_ref[...], og[:, :A_Q]).astype(BF16)
    nb = _rms(ob, og[:, A_Q:A_Q + B_OUT]).astype(BF16)
    nc = _rms(oc_ref[...], og[:, A_Q + B_OUT:]).astype(BF16)
    mix = (jnp.dot(na, w_ref[:A_Q], preferred_element_type=F32)
           + jnp.dot(nb, w_ref[A_Q:A_Q + B_OUT], preferred_element_type=F32)
           + jnp.dot(nc, w_ref[A_Q + B_OUT:], preferred_element_type=F32))
    x1 = x_ref[...] + mix
    x1_ref[...] = x1
    hn = _rms(x1, nf_ref[...])
    hn_ref[...] = hn.astype(BF16)
    logits = jnp.dot(hn, wr_ref[...], preferred_element_type=F32, precision=lax.Precision.HIGHEST)
    p = jnp.exp(logits - jnp.max(logits, axis=-1, keepdims=True))
    aff_ref[...] = p / jnp.sum(p, axis=-1, keepdims=True)


def out_proj(oa, obs, oc, x, out_gain, w_out_bf16, norm_ffn, w_router, seq, tm=512):
    n = x.shape[0]
    nt = seq // tm
    row = lambda w: pl.BlockSpec((tm, w), lambda i: (i, 0))
    full = lambda a: pl.BlockSpec(a.shape, lambda i: (0,) * a.ndim)
    sub = [pl.BlockSpec((1, d, tm // d, B_OUT), lambda i: (i // nt, 0, i % nt, 0))
           for _, d in DILATED_PAIRS]
    og = out_gain.reshape(1, D_MIX)
    nf = norm_ffn.reshape(1, D_MODEL)
    return pl.pallas_call(
        _outproj_kernel,
        grid=(n // tm,),
        in_specs=[row(A_Q)] + sub + sub + [row(C_W), row(D_MODEL),
                  full(og), full(w_out_bf16), full(nf), full(w_router)],
        out_specs=[row(D_MODEL), row(D_MODEL), row(N_EXPERTS)],
        out_shape=[jax.ShapeDtypeStruct((n, D_MODEL), F32),
                   jax.ShapeDtypeStruct((n, D_MODEL), BF16),
                   jax.ShapeDtypeStruct((n, N_EXPERTS), F32)],
        scratch_shapes=[pltpu.VMEM((B_OUT // LANES, tm, LANES), F32)] * (2 * B_GROUPS),
        compiler_params=_cparams(("arbitrary",)),
        name="out_proj",
    )(oa, obs[0][0], obs[1][0], obs[2][0], obs[0][1], obs[1][1], obs[2][1], oc, x,
      og, w_out_bf16, nf, w_router)


ROUTE_BLOCK = LANES


def _route_kernel(aff_ref, lpos_ref, cnt_ref, off_ref, base_ref, base_acc, *, cap):
    e = pl.program_id(0)

    @pl.when(e == 0)
    def _():
        base_acc[...] = jnp.zeros(base_acc.shape, F32)

    bits = lax.bitcast_convert_type(aff_ref[0], jnp.int32)
    nb = bits.shape[0]

    def search(i, thr):
        cand = thr | jnp.left_shift(jnp.int32(1), 30 - i)
        cnt = jnp.sum((bits >= cand).astype(F32), keepdims=True)
        return jnp.where(cnt >= cap, cand, thr)

    thr = lax.fori_loop(0, 31, search, jnp.zeros((1, 1), jnp.int32))
    gt = bits > thr
    eq = bits == thr
    need = cap - jnp.sum(gt.astype(F32), keepdims=True)

    ri = lax.broadcasted_iota(jnp.int32, (LANES, LANES), 0)
    ci = lax.broadcasted_iota(jnp.int32, (LANES, LANES), 1)
    upper = (ri <= ci).astype(F32).astype(BF16)
    ones = jnp.ones((LANES, LANES), BF16)
    rr = lax.broadcasted_iota(jnp.int32, (nb, nb), 0)
    rc = lax.broadcasted_iota(jnp.int32, (nb, nb), 1)
    strict_lower = (rc < rr).astype(F32).astype(BF16)

    def prefix(mask_f32):
        mb = mask_f32.astype(BF16)
        in_row = jnp.dot(mb, upper, preferred_element_type=F32)
        tot = jnp.dot(mb, ones, preferred_element_type=F32)
        offs = jnp.dot(strict_lower, tot.astype(BF16), preferred_element_type=F32)
        return in_row, tot, offs

    eq_f = eq.astype(F32)
    in_row_e, _, offs_e = prefix(eq_f)
    rank_eq = offs_e + in_row_e - eq_f
    sel = gt | (eq & (rank_eq < need))
    in_row, tot, offs = prefix(sel.astype(F32))
    base = base_acc[...]
    lpos_ref[0] = jnp.where(sel, base + in_row - 1.0, -1.0).astype(jnp.int32)
    cnt_ref[0] = tot.astype(jnp.int32)
    off_ref[0] = offs.astype(jnp.int32)
    base_ref[0] = base.astype(jnp.int32)
    base_acc[...] = base + tot


def route(aff, cap):
    n = aff.shape[0]
    nb = n // LANES
    aff_t = aff.T.reshape(N_EXPERTS, nb, LANES)
    spec = pl.BlockSpec((1, nb, LANES), lambda e: (e, 0, 0))
    outs = pl.pallas_call(
        functools.partial(_route_kernel, cap=cap),
        grid=(N_EXPERTS,),
        in_specs=[spec],
        out_specs=[spec] * 4,
        out_shape=[jax.ShapeDtypeStruct((N_EXPERTS, nb, LANES), jnp.int32)] * 4,
        scratch_shapes=[pltpu.VMEM((nb, LANES), F32)],
        compiler_params=_cparams(("arbitrary",)),
        name="route",
    )(aff_t)
    lpos, cnt, off, base = outs
    scal = lambda a: a[:, :, 0].T.reshape(-1)
    by_block = lambda a: a.transpose(1, 0, 2)
    return by_block(lpos), by_block(aff_t), scal(cnt), scal(off), scal(base)


ROW_TILE = (D_MODEL // LANES, LANES)
SEG_PIECE = 16
SEG_BITS = tuple(1 << k for k in range(3, -1, -1))
MAX_BLOCK_ROWS = N_EXPERTS * ROUTE_BLOCK


def _segment_copies(fn, cnt_s, blk, make):
    for e in range(N_EXPERTS):
        c = cnt_s[blk * N_EXPERTS + e]
        npiece = c // SEG_PIECE

        def piece(i, carry):
            fn(make(e, i * SEG_PIECE, SEG_PIECE))
            return carry

        lax.fori_loop(0, npiece, piece, 0)
        rem = c - npiece * SEG_PIECE
        for bit in SEG_BITS:
            done = npiece * SEG_PIECE + (rem // (2 * bit)) * (2 * bit)

            @pl.when((rem & bit) != 0)
            def _():
                fn(make(e, done, bit))


def _start(cp):
    cp.start()


def _wait(cp):
    cp.wait()


def _block_rows(cnt_s, base_s, blk):
    last = blk * N_EXPERTS + N_EXPERTS - 1
    return base_s[last] + cnt_s[last]


def _dispatch_kernel(cnt_s, off_s, base_s, h_ref, lpos_ref, xe_hbm, xc_ref, sem, *, cap):
    b = pl.program_id(0)
    nblk = pl.num_programs(0)
    slot = b % 2
    h = h_ref[...]
    lp = lpos_ref[0]

    def chunk(k, carry):
        r0 = pl.multiple_of(k * LANES, LANES)
        rho = r0 + lax.broadcasted_iota(jnp.int32, (LANES, ROUTE_BLOCK), 0)
        hit = lp[0:1, :] == rho
        for e in range(1, N_EXPERTS):
            hit = hit | (lp[e:e + 1, :] == rho)
        onehot = hit.astype(F32).astype(BF16)
        for f in range(ROW_TILE[0]):
            xc_ref[slot, pl.ds(r0, LANES), f, :] = jnp.dot(
                onehot, h[:, f * LANES:(f + 1) * LANES], preferred_element_type=F32)
        return carry

    lax.fori_loop(0, (_block_rows(cnt_s, base_s, b) + LANES - 1) // LANES, chunk, 0)

    def copies(blk, sl):
        def make(e, done, rows):
            src = base_s[blk * N_EXPERTS + e] + done
            dst = e * cap + off_s[blk * N_EXPERTS + e] + done
            return pltpu.make_async_copy(xc_ref.at[sl, pl.ds(src, rows)],
                                         xe_hbm.at[pl.ds(dst, rows)], sem.at[sl])
        return make

    _segment_copies(_start, cnt_s, b, copies(b, slot))

    @pl.when(b > 0)
    def _():
        _segment_copies(_wait, cnt_s, b - 1, copies(b - 1, 1 - slot))

    @pl.when(b == nblk - 1)
    def _():
        _segment_copies(_wait, cnt_s, b, copies(b, slot))


def dispatch(hn, lpos_t, cnt, off, base, cap):
    n = hn.shape[0]
    nb = n // ROUTE_BLOCK
    grid_spec = pltpu.PrefetchScalarGridSpec(
        num_scalar_prefetch=3,
        grid=(nb,),
        in_specs=[pl.BlockSpec((ROUTE_BLOCK, D_MODEL), lambda b, *_: (b, 0)),
                  pl.BlockSpec((1, N_EXPERTS, ROUTE_BLOCK), lambda b, *_: (b, 0, 0))],
        out_specs=pl.BlockSpec(memory_space=pl.ANY),
        scratch_shapes=[pltpu.VMEM((2, MAX_BLOCK_ROWS) + ROW_TILE, F32),
                        pltpu.SemaphoreType.DMA((2,))],
    )
    return pl.pallas_call(
        functools.partial(_dispatch_kernel, cap=cap),
        grid_spec=grid_spec,
        out_shape=jax.ShapeDtypeStruct((N_EXPERTS * cap,) + ROW_TILE, F32),
        compiler_params=_cparams(("arbitrary",)),
        name="dispatch",
    )(cnt, off, base, hn, lpos_t)


def _ffn_kernel(xe_ref, wg_ref, wu_ref, wd_ref, ye_ref, xs_ref, *, hid_chunk):
    m = xs_ref.shape[0]
    nf = ROW_TILE[0]
    for f in range(nf):
        xs_ref[:, f * LANES:(f + 1) * LANES] = xe_ref[pl.ds(f, m, stride=nf), :].astype(BF16)
    x = xs_ref[...]
    acc = jnp.zeros((m, D_MODEL), F32)
    for c in range(D_EXPERT // hid_chunk):
        cols = slice(c * hid_chunk, (c + 1) * hid_chunk)
        g = jnp.dot(x, wg_ref[0, :, cols], preferred_element_type=F32)
        u = jnp.dot(x, wu_ref[0, :, cols], preferred_element_type=F32)
        hid = (g * jax.nn.sigmoid(g) * u).astype(BF16)
        acc = acc + jnp.dot(hid, wd_ref[0, cols, :], preferred_element_type=F32)
    for f in range(nf):
        ye_ref[pl.ds(f, m, stride=nf), :] = acc[:, f * LANES:(f + 1) * LANES]


def expert_ffn(xe, wg, wu, wd, cap, tm=512, hid_chunk=512):
    nf = ROW_TILE[0]
    rows = xe.shape[0]
    tiles = cap // tm
    xe2 = xe.reshape(rows * nf, LANES)
    io_spec = pl.BlockSpec((tm * nf, LANES), lambda e, j: (e * tiles + j, 0))
    ye = pl.pallas_call(
        functools.partial(_ffn_kernel, hid_chunk=hid_chunk),
        grid=(N_EXPERTS, tiles),
        in_specs=[io_spec,
                  pl.BlockSpec((1, D_MODEL, D_EXPERT), lambda e, j: (e, 0, 0)),
                  pl.BlockSpec((1, D_MODEL, D_EXPERT), lambda e, j: (e, 0, 0)),
                  pl.BlockSpec((1, D_EXPERT, D_MODEL), lambda e, j: (e, 0, 0))],
        out_specs=io_spec,
        out_shape=jax.ShapeDtypeStruct((rows * nf, LANES), F32),
        scratch_shapes=[pltpu.VMEM((tm, D_MODEL), BF16)],
        compiler_params=_cparams(("arbitrary", "arbitrary")),
        name="expert_ffn",
    )(xe2, wg, wu, wd)
    return ye.reshape((rows,) + ROW_TILE)


def _combine_kernel(cnt_s, off_s, base_s, x_ref, lpos_ref, aff_ref, fg_ref, ye_hbm, o_ref,
                    yc_ref, sem, *, cap, final):
    b = pl.program_id(0)
    nblk = pl.num_programs(0)
    slot = b % 2

    def copies(blk, sl):
        def make(e, done, rows):
            src = e * cap + off_s[blk * N_EXPERTS + e] + done
            dst = base_s[blk * N_EXPERTS + e] + done
            return pltpu.make_async_copy(ye_hbm.at[pl.ds(src, rows)],
                                         yc_ref.at[sl, pl.ds(dst, rows)], sem.at[sl])
        return make

    @pl.when(b == 0)
    def _():
        yc_ref[...] = jnp.zeros(yc_ref.shape, F32)
        _segment_copies(_start, cnt_s, 0, copies(0, 0))

    @pl.when(b + 1 < nblk)
    def _():
        _segment_copies(_start, cnt_s, b + 1, copies(b + 1, 1 - slot))

    _segment_copies(_wait, cnt_s, b, copies(b, slot))

    lp = lpos_ref[0]
    af = aff_ref[0]
    o_ref[...] = x_ref[...]

    def chunk(k, carry):
        r0 = pl.multiple_of(k * LANES, LANES)
        rho = r0 + lax.broadcasted_iota(jnp.int32, (LANES, ROUTE_BLOCK), 0)
        wt = jnp.zeros((LANES, ROUTE_BLOCK), F32)
        for e in range(N_EXPERTS):
            wt = jnp.where(lp[e:e + 1, :] == rho, af[e:e + 1, :], wt)
        w = wt.T
        hi = w.astype(BF16)
        lo = (w - hi.astype(F32)).astype(BF16)
        for f in range(ROW_TILE[0]):
            y = yc_ref[slot, pl.ds(r0, LANES), f, :].astype(BF16)
            o_ref[:, f * LANES:(f + 1) * LANES] += (
                jnp.dot(hi, y, preferred_element_type=F32)
                + jnp.dot(lo, y, preferred_element_type=F32))
        return carry

    lax.fori_loop(0, (_block_rows(cnt_s, base_s, b) + LANES - 1) // LANES, chunk, 0)
    if final:
        o_ref[...] = _rms(o_ref[...], fg_ref[...])


def combine(x1, lpos_t, aff_t, final_gain, ye, cnt, off, base, cap, final):
    n = x1.shape[0]
    nb = n // ROUTE_BLOCK
    by_block = pl.BlockSpec((1, N_EXPERTS, ROUTE_BLOCK), lambda b, *_: (b, 0, 0))
    grid_spec = pltpu.PrefetchScalarGridSpec(
        num_scalar_prefetch=3,
        grid=(nb,),
        in_specs=[pl.BlockSpec((ROUTE_BLOCK, D_MODEL), lambda b, *_: (b, 0)),
                  by_block, by_block,
                  pl.BlockSpec((1, D_MODEL), lambda b, *_: (0, 0)),
                  pl.BlockSpec(memory_space=pl.ANY)],
        out_specs=pl.BlockSpec((ROUTE_BLOCK, D_MODEL), lambda b, *_: (b, 0)),
        scratch_shapes=[pltpu.VMEM((2, MAX_BLOCK_ROWS) + ROW_TILE, F32),
                        pltpu.SemaphoreType.DMA((2,))],
    )
    return pl.pallas_call(
        functools.partial(_combine_kernel, cap=cap, final=final),
        grid_spec=grid_spec,
        out_shape=jax.ShapeDtypeStruct((n, D_MODEL), F32),
        compiler_params=_cparams(("arbitrary",)),
        name="combine",
    )(cnt, off, base, x1, lpos_t, aff_t, final_gain.reshape(1, D_MODEL), ye)


def moe(x1, hn, aff, wg, wu, wd, final_gain, final):
    n = x1.shape[0]
    cap = EC_CAPACITY * n // N_EXPERTS
    lpos_t, aff_t, cnt, off, base = route(aff, cap)
    xe = dispatch(hn, lpos_t, cnt, off, base, cap)
    ye = expert_ffn(xe, wg, wu, wd, cap)
    return combine(x1, lpos_t, aff_t, final_gain, ye, cnt, off, base, cap, final)


def _trunk(x, p):
    batch, seq, _ = x.shape
    n = batch * seq
    x = x.reshape(n, D_MODEL)
    for l in range(DEPTH):
        a_arr, c_arr, *groups = in_proj(x, p["norm_mix"][l], p["w_in"][l], batch, seq)
        qx, k2, vx = mixer_a_prep(a_arr, p["q_gain"][l], p["k_gain"][l], batch, seq)
        oa = mixer_a(qx, k2, vx, batch, seq)
        obs = [mixer_b_group(groups[g], p["rel_bias"], g) for g in range(B_GROUPS)]
        oc = mixer_c(c_arr, p["na_rpb"][l], batch, seq)
        x1, hn, aff = out_proj(oa, obs, oc, x, p["out_gain"][l], p["w_out"][l],
                               p["norm_ffn"][l], p["w_router"][l], seq)
        x = moe(x1, hn, aff, p["w_gate"][l], p["w_up"][l], p["w_down"][l],
                p["final_norm"], final=(l == DEPTH - 1))
    return x.reshape(batch, seq, D_MODEL)


def kernel(x_prompt, x_sample, w_in, w_out, norm_mix, norm_ffn, q_gain, k_gain, out_gain,
           na_rpb, rel_bias, w_router, w_gate, w_up, w_down, final_norm):
    p = dict(w_in=permute_w_in(w_in).astype(BF16), w_out=w_out.astype(BF16), norm_mix=norm_mix,
             norm_ffn=norm_ffn, q_gain=q_gain, k_gain=k_gain, out_gain=out_gain,
             na_rpb=na_rpb, rel_bias=rel_bias, w_router=w_router,
             w_gate=w_gate.astype(BF16), w_up=w_up.astype(BF16), w_down=w_down.astype(BF16),
             final_norm=final_norm)
    return (_trunk(x_prompt, p), _trunk(x_sample, p))
```

```python
import functools
import math

import jax
import jax.numpy as jnp
import numpy as np
from jax import lax
from jax.experimental import pallas as pl
from jax.experimental.pallas import tpu as pltpu

D_MODEL = 1024
DEPTH = 2
GRID_W = 64
HEAD_DIM = 64
EPS = 1e-6
A_HEADS = 8
A_KV_HEADS = 2
A_GROUP = A_HEADS // A_KV_HEADS
ROPE_THETA = 10000.0
DILATED_PAIRS = ((128, 1), (512, 4), (2048, 16))
B_GROUPS = 3
B_HEADS = 4
C_HEADS = 4
NA_ROWS = 8
NA_COLS = 16
N_BUCKETS = 32
MAX_DISTANCE = 2048
N_EXPERTS = 16
EC_CAPACITY = 2
D_EXPERT = 2 * D_MODEL
A_Q = A_HEADS * HEAD_DIM
A_KV = A_KV_HEADS * HEAD_DIM
B_OUT = B_HEADS * HEAD_DIM
C_W = C_HEADS * HEAD_DIM
D_IN = 3840
D_MIX = A_Q + B_OUT + C_W

LANES = 128
PROJ_CHUNK = 768
NEG = -1e30
LOG2E = 1.4426950408889634
VMEM_LIMIT = 56 * 1024 * 1024

F32 = jnp.float32
BF16 = jnp.bfloat16


def _cparams(sem):
    return pltpu.CompilerParams(dimension_semantics=sem, vmem_limit_bytes=VMEM_LIMIT)


def permute_w_in(w_in):
    b0 = A_Q + 2 * A_KV
    bw = B_GROUPS * B_OUT
    parts = [w_in[..., :b0], w_in[..., b0 + 3 * bw:]]
    for g in range(B_GROUPS):
        parts += [w_in[..., b0 + s * bw + g * B_OUT:b0 + s * bw + (g + 1) * B_OUT] for s in range(3)]
    return jnp.concatenate(parts, axis=-1)


def _inproj_kernel(x_ref, g_ref, w_ref, a_ref, c_ref, *rest):
    g_refs, scr_ref = rest[:B_GROUPS], rest[B_GROUPS]
    x = x_ref[...]
    tm = x.shape[0]
    ms = jnp.mean(x * x, axis=-1, keepdims=True)
    h = (x * lax.rsqrt(ms + EPS) * g_ref[...]).astype(BF16)

    def proj(k):
        return jnp.dot(h, w_ref[:, k * PROJ_CHUNK:(k + 1) * PROJ_CHUNK], preferred_element_type=F32)

    a_ref[...] = proj(0).astype(BF16)
    c_ref[...] = proj(1).astype(BF16)
    for g, g_ref_out in enumerate(g_refs):
        d = DILATED_PAIRS[g][1]
        if d == 1:
            g_ref_out[0, 0] = proj(2 + g).astype(BF16)
        else:
            res = proj(2 + g)
            ntile = PROJ_CHUNK // LANES
            for c in range(ntile):
                scr_ref[c] = res[:, c * LANES:(c + 1) * LANES]
            for r in range(d):
                for c in range(ntile):
                    g_ref_out[0, r, :, c * LANES:(c + 1) * LANES] = (
                        scr_ref[c, pl.ds(r, tm // d, stride=d), :].astype(BF16))


def in_proj(x, gain, w_perm_bf16, batch, seq, tm=512):
    n = batch * seq
    nt = seq // tm
    row = pl.BlockSpec((tm, PROJ_CHUNK), lambda i: (i, 0))
    g_specs, g_shapes = [], []
    for _, d in DILATED_PAIRS:
        g_specs.append(pl.BlockSpec((1, d, tm // d, PROJ_CHUNK), lambda i: (i // nt, 0, i % nt, 0)))
        g_shapes.append(jax.ShapeDtypeStruct((batch, d, seq // d, PROJ_CHUNK), BF16))
    return pl.pallas_call(
        _inproj_kernel,
        grid=(n // tm,),
        in_specs=[pl.BlockSpec((tm, D_MODEL), lambda i: (i, 0)),
                  pl.BlockSpec((1, D_MODEL), lambda i: (0, 0)),
                  pl.BlockSpec((D_MODEL, D_IN), lambda i: (0, 0))],
        out_specs=[row, row] + g_specs,
        out_shape=[jax.ShapeDtypeStruct((n, PROJ_CHUNK), BF16)] * 2 + g_shapes,
        scratch_shapes=[pltpu.VMEM((PROJ_CHUNK // LANES, tm, LANES), F32)],
        compiler_params=_cparams(("arbitrary",)),
        name="in_proj",
    )(x, gain.reshape(1, D_MODEL), w_perm_bf16)


def _rope_tables(seq):
    t = np.arange(seq)
    row = (t // GRID_W).astype(np.float64)
    col = (t % GRID_W).astype(np.float64)
    half = HEAD_DIM // 2
    freqs = ROPE_THETA ** (-np.arange(0, half, 2, dtype=np.float64) / half)
    ang = np.concatenate([row[:, None] * freqs, col[:, None] * freqs], axis=-1)
    cos = np.repeat(np.cos(ang), 2, axis=-1)
    sin = np.repeat(np.sin(ang), 2, axis=-1)
    sign = np.tile(np.array([-1.0, 1.0], np.float32), HEAD_DIM // 2)
    cos2 = np.tile(cos, (1, 2)).astype(np.float32)
    sin2 = np.tile(sin * sign, (1, 2)).astype(np.float32)
    return jnp.asarray(cos2), jnp.asarray(sin2)


def _headnorm_rope(x, gain2, cos, sin, seg):
    sq = x * x
    hi = sq.astype(BF16)
    lo = (sq - hi.astype(F32)).astype(BF16)
    ms = (jnp.dot(hi, seg, preferred_element_type=F32)
          + jnp.dot(lo, seg, preferred_element_type=F32))
    y = x * lax.rsqrt(ms + EPS) * gain2
    lane = lax.broadcasted_iota(jnp.int32, y.shape, 1)
    even = (lane % 2) == 0
    swapped = jnp.where(even, pltpu.roll(y, LANES - 1, 1), pltpu.roll(y, 1, 1))
    return y * cos + swapped * sin


def _aprep_kernel(a_ref, cos_ref, sin_ref, qg_ref, kg_ref, q_ref, k_ref, v_ref):
    cos = cos_ref[...]
    sin = sin_ref[...]
    r = lax.broadcasted_iota(jnp.int32, (LANES, LANES), 0) // HEAD_DIM
    c = lax.broadcasted_iota(jnp.int32, (LANES, LANES), 1) // HEAD_DIM
    seg = jnp.where(r == c, 1.0 / HEAD_DIM, 0.0).astype(BF16)
    tm = a_ref.shape[0]
    lane = lax.broadcasted_iota(jnp.int32, (tm, LANES), 1)
    low = lane < HEAD_DIM
    qscale = (HEAD_DIM ** -0.5) * LOG2E
    for t in range(A_Q // LANES):
        x = a_ref[:, t * LANES:(t + 1) * LANES].astype(F32)
        y = _headnorm_rope(x, qg_ref[...], cos, sin, seg) * qscale
        swapped = pltpu.roll(y, HEAD_DIM, 1)
        if (2 * t) // A_GROUP == 0:
            q_even, q_odd = jnp.where(low, y, 0.0), jnp.where(low, swapped, 0.0)
        else:
            q_even, q_odd = jnp.where(low, 0.0, swapped), jnp.where(low, 0.0, y)
        q_ref[0, 2 * t] = q_even.astype(BF16)
        q_ref[0, 2 * t + 1] = q_odd.astype(BF16)
    xk = a_ref[:, A_Q:A_Q + LANES].astype(F32)
    k_ref[0] = _headnorm_rope(xk, kg_ref[...], cos, sin, seg).astype(BF16)
    xv = a_ref[:, A_Q + LANES:A_Q + 2 * LANES].astype(F32)
    one_col = (lane == HEAD_DIM).astype(F32)
    v_ref[0, 0] = (jnp.where(low, xv, 0.0) + one_col).astype(BF16)
    v_ref[0, 1] = (jnp.where(low, pltpu.roll(xv, HEAD_DIM, 1), 0.0) + one_col).astype(BF16)


def mixer_a_prep(a_arr, q_gain, k_gain, batch, seq, tm=512):
    cos2, sin2 = _rope_tables(seq)
    nb = seq // tm
    qg = jnp.tile(q_gain.astype(F32), 2).reshape(1, LANES)
    kg = jnp.tile(k_gain.astype(F32), 2).reshape(1, LANES)
    return pl.pallas_call(
        _aprep_kernel,
        grid=(batch, nb),
        in_specs=[pl.BlockSpec((tm, PROJ_CHUNK), lambda b, i: (b * nb + i, 0)),
                  pl.BlockSpec((tm, LANES), lambda b, i: (i, 0)),
                  pl.BlockSpec((tm, LANES), lambda b, i: (i, 0)),
                  pl.BlockSpec((1, LANES), lambda b, i: (0, 0)),
                  pl.BlockSpec((1, LANES), lambda b, i: (0, 0))],
        out_specs=[pl.BlockSpec((1, A_HEADS, tm, LANES), lambda b, i: (b, 0, i, 0)),
                   pl.BlockSpec((1, tm, LANES), lambda b, i: (b, i, 0)),
                   pl.BlockSpec((1, A_KV_HEADS, tm, LANES), lambda b, i: (b, 0, i, 0))],
        out_shape=[jax.ShapeDtypeStruct((batch, A_HEADS, seq, LANES), BF16),
                   jax.ShapeDtypeStruct((batch, seq, LANES), BF16),
                   jax.ShapeDtypeStruct((batch, A_KV_HEADS, seq, LANES), BF16)],
        compiler_params=_cparams(("arbitrary", "arbitrary")),
        name="mixer_a_prep",
    )(a_arr, cos2, sin2, qg, kg)


FLASH_ROWS = 256


def _flash_kernel(q_ref, k_ref, v_ref, o_ref, m_ref, acc_ref):
    j = pl.program_id(3)
    tq = q_ref.shape[2]
    tk = k_ref.shape[1]

    @pl.when(j == 0)
    def _():
        m_ref[...] = jnp.full(m_ref.shape, -jnp.inf, F32)
        acc_ref[...] = jnp.zeros(acc_ref.shape, F32)

    k = k_ref[0]
    v = v_ref[0, 0]
    for h in range(A_GROUP):
        for r0 in range(0, tq, FLASH_ROWS):
            rows = pl.ds(r0, FLASH_ROWS)
            s = lax.dot_general(q_ref[0, h, rows, :], k, (((1,), (1,)), ((), ())),
                                preferred_element_type=F32)
            m_prev = m_ref[h, rows, :]
            m_new = jnp.maximum(m_prev, jnp.max(s, axis=-1, keepdims=True))
            alpha = jnp.exp2(m_prev - m_new)
            p = jnp.exp2(s - jnp.tile(m_new, (1, tk // LANES))).astype(BF16)
            acc_ref[h, rows, :] = alpha * acc_ref[h, rows, :] + jnp.dot(
                p, v, preferred_element_type=F32)
            m_ref[h, rows, :] = m_new

    @pl.when(j == pl.num_programs(3) - 1)
    def _():
        lane = lax.broadcasted_iota(jnp.int32, (tq, LANES), 1)
        low = lane < HEAD_DIM

        def normed(h):
            acc = acc_ref[h]
            return acc / acc[:, HEAD_DIM:HEAD_DIM + 1]

        for t in range(A_GROUP // 2):
            o_ref[:, t * LANES:(t + 1) * LANES] = jnp.where(
                low, normed(2 * t), pltpu.roll(normed(2 * t + 1), HEAD_DIM, 1)).astype(o_ref.dtype)


def mixer_a(qx, k2, vx, batch, seq, tq=512, tk=2048):
    nq = seq // tq
    grp_w = A_GROUP * HEAD_DIM
    return pl.pallas_call(
        _flash_kernel,
        grid=(batch, A_KV_HEADS, nq, seq // tk),
        in_specs=[pl.BlockSpec((1, A_GROUP, tq, LANES), lambda b, h, i, j: (b, h, i, 0)),
                  pl.BlockSpec((1, tk, LANES), lambda b, h, i, j: (b, j, 0)),
                  pl.BlockSpec((1, 1, tk, LANES), lambda b, h, i, j: (b, h, j, 0))],
        out_specs=pl.BlockSpec((tq, grp_w), lambda b, h, i, j: (b * nq + i, h)),
        out_shape=jax.ShapeDtypeStruct((batch * seq, A_Q), F32),
        scratch_shapes=[pltpu.VMEM((A_GROUP, tq, LANES), F32),
                        pltpu.VMEM((A_GROUP, tq, LANES), F32)],
        compiler_params=_cparams(("arbitrary",) * 4),
        name="mixer_a_flash",
    )(qx, k2, vx)


B_TL = 128
B_CHAINS = 4


def _toeplitz(vec, nrow, ncol, shift):
    period = vec.shape[-1] + 1
    lead = vec.shape[:-1]
    padded = jnp.concatenate([vec, jnp.zeros(lead + (1,), vec.dtype)], axis=-1)
    flat = jnp.tile(padded, (1,) * len(lead) + (nrow,))[..., :nrow * (period - 1)]
    skew = flat.reshape(lead + (nrow, period - 1))
    return skew[..., shift:shift + ncol]
def _t5_bucket(rel):
    nb = N_BUCKETS // 2
    max_exact = nb // 2
    ret = (rel > 0).astype(jnp.int32) * nb
    n = jnp.abs(rel)
    large = max_exact + (jnp.log(jnp.maximum(n, 1).astype(jnp.float32) / max_exact)
                         / math.log(MAX_DISTANCE / max_exact) * (nb - max_exact)).astype(jnp.int32)
    large = jnp.minimum(large, nb - 1)
    return ret + jnp.where(n < max_exact, n, large)


def _dilated_bias(rel_bias_g, dilation, tl, radius):
    off = np.arange(2 * tl + 2 * radius - 1) - (tl - 1) - radius
    per_off = rel_bias_g[_t5_bucket(jnp.asarray(off * dilation))].astype(F32).T
    per_off = jnp.where(jnp.asarray(np.abs(off) <= radius)[None], per_off, NEG)
    return _toeplitz(per_off, tl, tl + 2 * radius, tl - 1)


def _head_mask(shape, h):
    lane = lax.broadcasted_iota(jnp.int32, shape, 1)
    return (lane < HEAD_DIM) if h % 2 == 0 else (lane >= HEAD_DIM)


def _windowed_heads(q, kwin, vwin, bias_ref, valid, want_lse):
    tl = q.shape[0]
    scale = jnp.asarray(HEAD_DIM ** -0.5, q.dtype)
    o_tiles, lse_tiles = [], []
    for t in range(2):
        qt = q[:, t * LANES:(t + 1) * LANES] * scale
        kt = kwin[:, t * LANES:(t + 1) * LANES]
        vt = vwin[:, t * LANES:(t + 1) * LANES]
        o_pair, lse_pair = [], []
        for hh in range(2):
            h = 2 * t + hh
            sel = _head_mask((tl, LANES), h)
            qm = jnp.where(sel, qt, jnp.zeros_like(qt))
            s = lax.dot_general(qm, kt, (((1,), (1,)), ((), ())), preferred_element_type=F32)
            s = s + bias_ref[h]
            if valid is not None:
                s = jnp.where(valid, s, NEG)
            m = jnp.max(s, axis=-1, keepdims=True)
            p = jnp.exp(s - m)
            den = jnp.sum(p, axis=-1, keepdims=True)
            pv = jnp.dot(p.astype(BF16), vt, preferred_element_type=F32)
            o_pair.append(pv / den)
            lse_pair.append(m + jnp.log(den))
        sel0 = _head_mask((tl, LANES), 0)
        o_tiles.append(jnp.where(sel0, o_pair[0], o_pair[1]))
        if want_lse:
            lse_tiles.append(jnp.where(sel0, lse_pair[0], lse_pair[1]))
    return o_tiles, lse_tiles


def _dilated_kernel(q_ref, kp_ref, km_ref, kn_ref, vp_ref, vm_ref, vn_ref, bias_ref,
                    o_ref, lse_ref, *, sub_len, radius):
    i = pl.program_id(2)
    nres, tls = q_ref.shape[0], q_ref.shape[1]
    nsub = tls // B_TL
    win = B_TL + 2 * radius
    for r in range(nres):
        kwin = jnp.concatenate([kp_ref[r], km_ref[r], kn_ref[r]], axis=0)
        vwin = jnp.concatenate([vp_ref[r], vm_ref[r], vn_ref[r]], axis=0)
        for j in range(nsub):
            valid = None
            if j == 0 or j == nsub - 1:
                kpos = (i * tls + j * B_TL - radius
                        + lax.broadcasted_iota(jnp.int32, (1, win), 1))
                valid = (kpos >= 0) & (kpos < sub_len)
            rows = slice(j * B_TL, (j + 1) * B_TL)
            o_tiles, lse_tiles = _windowed_heads(
                q_ref[r, rows, :], kwin[j * B_TL:j * B_TL + win], vwin[j * B_TL:j * B_TL + win],
                bias_ref, valid, True)
            for t in range(2):
                o_ref[r, rows, t * LANES:(t + 1) * LANES] = o_tiles[t]
                lse_ref[r, rows, t * LANES:(t + 1) * LANES] = lse_tiles[t]


def mixer_b_group(gqkv, rel_bias, g):
    batch, d, sub_len, _ = gqkv.shape
    window = DILATED_PAIRS[g][0]
    radius = window // (2 * d)
    assert sub_len % B_TL == 0 and B_TL % radius == 0
    nsub = min(B_CHAINS, sub_len // B_TL)
    nres = min(d, B_CHAINS // nsub)
    tls = nsub * B_TL
    hpb = tls // radius
    nhalo = sub_len // radius
    bias = _dilated_bias(rel_bias[:, g * B_HEADS:(g + 1) * B_HEADS], d, B_TL, radius)

    def main(col):
        return pl.BlockSpec((None, nres, tls, B_OUT), lambda b, r, i: (b, r, i, col))

    def prev(col):
        return pl.BlockSpec((None, nres, radius, B_OUT),
                            lambda b, r, i: (b, r, jnp.maximum(i * hpb - 1, 0), col))

    def nxt(col):
        return pl.BlockSpec((None, nres, radius, B_OUT),
                            lambda b, r, i: (b, r, jnp.minimum((i + 1) * hpb, nhalo - 1), col))

    return pl.pallas_call(
        functools.partial(_dilated_kernel, sub_len=sub_len, radius=radius),
        grid=(batch, d // nres, sub_len // tls),
        in_specs=[main(0), prev(1), main(1), nxt(1), prev(2), main(2), nxt(2),
                  pl.BlockSpec((B_HEADS, B_TL, B_TL + 2 * radius), lambda b, r, i: (0, 0, 0))],
        out_specs=[main(0), main(0)],
        out_shape=[jax.ShapeDtypeStruct((batch, d, sub_len, B_OUT), F32)] * 2,
        compiler_params=_cparams(("arbitrary",) * 3),
        name=f"mixer_b_g{g}",
    )(gqkv, gqkv, gqkv, gqkv, gqkv, gqkv, gqkv, bias)


C_QROWS = 4
C_TQ = C_QROWS * GRID_W


def _na_bias(rpb):
    u = np.arange(C_TQ)[:, None] // GRID_W
    cq = np.arange(C_TQ)[:, None] % GRID_W
    w = np.arange(3 * C_TQ)[None, :] // GRID_W
    ck = np.arange(3 * C_TQ)[None, :] % GRID_W
    cs = np.clip(cq - NA_COLS // 2, 0, GRID_W - NA_COLS)
    col_ok = (ck >= cs) & (ck < cs + NA_COLS)
    row_ok = [(w >= C_QROWS) & (w < C_QROWS + NA_ROWS) & (u >= 0),
              (w >= u) & (w < u + NA_ROWS),
              (w >= 0) & (w < NA_ROWS) & (u >= 0)]
    side = GRID_W - NA_COLS
    padded = jnp.pad(rpb.astype(F32), ((0, 0), (0, 0), (side, side)))
    blocks = _toeplitz(padded, GRID_W, GRID_W, GRID_W - 1)
    drow = np.clip((np.arange(3 * C_QROWS)[None, :] - C_QROWS) - np.arange(C_QROWS)[:, None]
                   + NA_ROWS - 1, 0, 2 * NA_ROWS - 2)
    tiles = jnp.take(blocks, jnp.asarray(drow.reshape(-1)), axis=1)
    tiles = tiles.reshape(C_HEADS, C_QROWS, 3 * C_QROWS, GRID_W, GRID_W)
    vals = tiles.transpose(0, 1, 3, 2, 4).reshape(C_HEADS, C_TQ, 3 * C_TQ)
    return jnp.stack([jnp.where(jnp.asarray(r & col_ok)[None], vals, NEG) for r in row_ok])


def _na_kernel(q_ref, kp_ref, kc_ref, kn_ref, vp_ref, vc_ref, vn_ref, bias_ref, o_ref):
    kwin = jnp.concatenate([kp_ref[...], kc_ref[...], kn_ref[...]], axis=0)
    vwin = jnp.concatenate([vp_ref[...], vc_ref[...], vn_ref[...]], axis=0)
    o_tiles, _ = _windowed_heads(q_ref[...], kwin, vwin, bias_ref.at[0], None, False)
    for t in range(2):
        o_ref[:, t * LANES:(t + 1) * LANES] = o_tiles[t]


def mixer_c(c_arr, rpb, batch, seq):
    rows = seq // GRID_W
    assert rows % C_QROWS == 0 and rows >= 3 * C_QROWS
    nblk = rows // C_QROWS
    n = batch * seq
    bias = _na_bias(rpb)

    def spec(col, shift):
        def imap(b, i):
            return (b * nblk + jnp.clip(i + shift, 0, nblk - 1), col)
        return pl.BlockSpec((C_TQ, C_W), imap)

    def bias_map(b, i):
        return (jnp.where(i == 0, 0, jnp.where(i == nblk - 1, 2, 1)), 0, 0, 0)

    return pl.pallas_call(
        _na_kernel,
        grid=(batch, nblk),
        in_specs=[spec(0, 0), spec(1, -1), spec(1, 0), spec(1, 1),
                  spec(2, -1), spec(2, 0), spec(2, 1),
                  pl.BlockSpec((1, C_HEADS, C_TQ, 3 * C_TQ), bias_map)],
        out_specs=pl.BlockSpec((C_TQ, C_W), lambda b, i: (b * nblk + i, 0)),
        out_shape=jax.ShapeDtypeStruct((n, C_W), F32),
        compiler_params=_cparams(("arbitrary", "arbitrary")),
        name="mixer_c",
    )(c_arr, c_arr, c_arr, c_arr, c_arr, c_arr, c_arr, bias)


def _rms(x, gain):
    ms = jnp.mean(x * x, axis=-1, keepdims=True)
    return x * lax.rsqrt(ms + EPS) * gain


def _token_order(src_ref, scr_ref):
    d, sub = src_ref.shape[1], src_ref.shape[2]
    if d == 1:
        return src_ref[0, 0]
    ntile = scr_ref.shape[0]
    for r in range(d):
        for c in range(ntile):
            scr_ref[c, pl.ds(r, sub, stride=d), :] = src_ref[0, r, :, c * LANES:(c + 1) * LANES]
    return jnp.concatenate([scr_ref[c] for c in range(ntile)], axis=1)


def _outproj_kernel(oa_ref, ob0_ref, ob1_ref, ob2_ref, l0_ref, l1_ref, l2_ref, oc_ref, x_ref,
                    og_ref, w_ref, nf_ref, wr_ref, x1_ref, hn_ref, aff_ref, *scr):
    ob0, ob1, ob2 = (_token_order(r, s) for r, s in zip((ob0_ref, ob1_ref, ob2_ref), scr[:3]))
    l0, l1, l2 = (_token_order(r, s) for r, s in zip((l0_ref, l1_ref, l2_ref), scr[3:]))
    m = jnp.maximum(jnp.maximum(l0, l1), l2)
    e0, e1, e2 = jnp.exp(l0 - m), jnp.exp(l1 - m), jnp.exp(l2 - m)
    ob = (e0 * ob0 + e1 * ob1 + e2 * ob2) / (e0 + e1 + e2)
    og = og_ref[...]
    na = _rms(oa_ref[...], og[:, :A_Q]).astype(BF16)
    nb = _rms(ob, og[:, A_Q:A_Q + B_OUT]).astype(BF16)
    nc = _rms(oc_ref[...], og[:, A_Q + B_OUT:]).astype(BF16)
    mix = (jnp.dot(na, w_ref[:A_Q], preferred_element_type=F32)
           + jnp.dot(nb, w_ref[A_Q:A_Q + B_OUT], preferred_element_type=F32)
           + jnp.dot(nc, w_ref[A_Q + B_OUT:], preferred_element_type=F32))
    x1 = x_ref[...] + mix
    x1_ref[...] = x1
    hn = _rms(x1, nf_ref[...])
    hn_ref[...] = hn.astype(BF16)
    wr = wr_ref[...]
    h_hi, w_hi = hn.astype(BF16), wr.astype(BF16)
    h_lo = (hn - h_hi.astype(F32)).astype(BF16)
    w_lo = (wr - w_hi.astype(F32)).astype(BF16)
    logits = (jnp.dot(h_hi, w_hi, preferred_element_type=F32)
              + jnp.dot(h_lo, w_hi, preferred_element_type=F32)
              + jnp.dot(h_hi, w_lo, preferred_element_type=F32))
    p = jnp.exp(logits - jnp.max(logits, axis=-1, keepdims=True))
    aff_ref[...] = p / jnp.sum(p, axis=-1, keepdims=True)


def out_proj(oa, obs, oc, x, out_gain, w_out_bf16, norm_ffn, w_router, seq, tm=512):
    n = x.shape[0]
    nt = seq // tm
    row = lambda w: pl.BlockSpec((tm, w), lambda i: (i, 0))
    full = lambda a: pl.BlockSpec(a.shape, lambda i: (0,) * a.ndim)
    sub = [pl.BlockSpec((1, d, tm // d, B_OUT), lambda i: (i // nt, 0, i % nt, 0))
           for _, d in DILATED_PAIRS]
    og = out_gain.reshape(1, D_MIX)
    nf = norm_ffn.reshape(1, D_MODEL)
    return pl.pallas_call(
        _outproj_kernel,
        grid=(n // tm,),
        in_specs=[row(A_Q)] + sub + sub + [row(C_W), row(D_MODEL),
                  full(og), full(w_out_bf16), full(nf), full(w_router)],
        out_specs=[row(D_MODEL), row(D_MODEL), row(N_EXPERTS)],
        out_shape=[jax.ShapeDtypeStruct((n, D_MODEL), F32),
                   jax.ShapeDtypeStruct((n, D_MODEL), BF16),
                   jax.ShapeDtypeStruct((n, N_EXPERTS), F32)],
        scratch_shapes=[pltpu.VMEM((B_OUT // LANES, tm, LANES), F32)] * (2 * B_GROUPS),
        compiler_params=_cparams(("arbitrary",)),
        name="out_proj",
    )(oa, obs[0][0], obs[1][0], obs[2][0], obs[0][1], obs[1][1], obs[2][1], oc, x,
      og, w_out_bf16, nf, w_router)


ROUTE_BLOCK = 2 * LANES
ROUTE_ROWS = ROUTE_BLOCK // LANES


def _route_kernel(aff_ref, lpos_ref, cnt_ref, off_ref, base_ref, base_acc, *, cap):
    e = pl.program_id(0)

    @pl.when(e == 0)
    def _():
        base_acc[...] = jnp.zeros(base_acc.shape, F32)

    bits = lax.bitcast_convert_type(aff_ref[0], jnp.int32)
    nb = bits.shape[0]

    def search(i, thr):
        cand = thr | jnp.left_shift(jnp.int32(1), 30 - i)
        cnt = jnp.sum((bits >= cand).astype(F32), keepdims=True)
        return jnp.where(cnt >= cap, cand, thr)

    thr = lax.fori_loop(0, 31, search, jnp.zeros((1, 1), jnp.int32))
    gt = bits > thr
    eq = bits == thr
    need = cap - jnp.sum(gt.astype(F32), keepdims=True)

    ri = lax.broadcasted_iota(jnp.int32, (LANES, LANES), 0)
    ci = lax.broadcasted_iota(jnp.int32, (LANES, LANES), 1)
    upper = (ri <= ci).astype(F32).astype(BF16)
    ones = jnp.ones((LANES, LANES), BF16)
    rr = lax.broadcasted_iota(jnp.int32, (nb, nb), 0)
    rc = lax.broadcasted_iota(jnp.int32, (nb, nb), 1)
    strict_lower = (rc < rr).astype(F32).astype(BF16)
    same_block = (rc // ROUTE_ROWS) == (rr // ROUTE_ROWS)
    block_all = same_block.astype(F32).astype(BF16)
    block_lower = (same_block & (rc < rr)).astype(F32).astype(BF16)

    def prefix(mask_f32):
        mb = mask_f32.astype(BF16)
        in_row = jnp.dot(mb, upper, preferred_element_type=F32)
        tot = jnp.dot(mb, ones, preferred_element_type=F32)
        offs = jnp.dot(strict_lower, tot.astype(BF16), preferred_element_type=F32)
        return in_row, tot, offs

    eq_f = eq.astype(F32)
    in_row_e, _, offs_e = prefix(eq_f)
    rank_eq = offs_e + in_row_e - eq_f
    sel = gt | (eq & (rank_eq < need))
    in_row, tot, offs = prefix(sel.astype(F32))
    tot_b = tot.astype(BF16)
    in_block = jnp.dot(block_lower, tot_b, preferred_element_type=F32)
    block_tot = jnp.dot(block_all, tot_b, preferred_element_type=F32)
    base = base_acc[...]
    lpos_ref[0] = jnp.where(sel, base + in_block + in_row - 1.0, -1.0).astype(jnp.int32)
    cnt_ref[0] = block_tot.astype(jnp.int32)
    off_ref[0] = (offs - in_block).astype(jnp.int32)
    base_ref[0] = base.astype(jnp.int32)
    base_acc[...] = base + block_tot


def route(aff, cap):
    n = aff.shape[0]
    nb = n // LANES
    aff_t = aff.T.reshape(N_EXPERTS, nb, LANES)
    spec = pl.BlockSpec((1, nb, LANES), lambda e: (e, 0, 0))
    outs = pl.pallas_call(
        functools.partial(_route_kernel, cap=cap),
        grid=(N_EXPERTS,),
        in_specs=[spec],
        out_specs=[spec] * 4,
        out_shape=[jax.ShapeDtypeStruct((N_EXPERTS, nb, LANES), jnp.int32)] * 4,
        scratch_shapes=[pltpu.VMEM((nb, LANES), F32)],
        compiler_params=_cparams(("arbitrary",)),
        name="route",
    )(aff_t)
    lpos, cnt, off, base = outs
    nblk = n // ROUTE_BLOCK
    scal = lambda a: a[:, ::ROUTE_ROWS, 0].T.reshape(-1)
    by_block = lambda a: a.reshape(N_EXPERTS, nblk, ROUTE_BLOCK).transpose(1, 0, 2)
    return by_block(lpos), by_block(aff_t), scal(cnt), scal(off), scal(base)


ROW_TILE = (D_MODEL // LANES, LANES)


def _tile_rows(start, rows):
    nf = ROW_TILE[0]
    return pl.ds(pl.multiple_of(start * nf, nf), rows * nf)


def _feature_rows(start, rows, f):
    nf = ROW_TILE[0]
    return pl.ds(start * nf + f, rows, stride=nf)
SEG_PIECE = 16
SEG_BITS = tuple(1 << k for k in range(3, -1, -1))
MAX_BLOCK_ROWS = N_EXPERTS * ROUTE_BLOCK


def _segment_copies(fn, cnt_s, blk, make):
    for e in range(N_EXPERTS):
        c = cnt_s[blk * N_EXPERTS + e]
        npiece = c // SEG_PIECE

        def piece(i, carry):
            fn(make(e, i * SEG_PIECE, SEG_PIECE))
            return carry

        lax.fori_loop(0, npiece, piece, 0)
        rem = c - npiece * SEG_PIECE
        for bit in SEG_BITS:
            done = npiece * SEG_PIECE + (rem // (2 * bit)) * (2 * bit)

            @pl.when((rem & bit) != 0)
            def _():
                fn(make(e, done, bit))


def _start(cp):
    cp.start()


def _wait(cp):
    cp.wait()


def _block_rows(cnt_s, base_s, blk):
    last = blk * N_EXPERTS + N_EXPERTS - 1
    return base_s[last] + cnt_s[last]


def _chunk_experts(cnt_s, base_s, blk, r0):
    lo = jnp.int32(0)
    hi = jnp.int32(0)
    for e in range(N_EXPERTS):
        start = base_s[blk * N_EXPERTS + e]
        end = start + cnt_s[blk * N_EXPERTS + e]
        lo = lo + (end <= r0).astype(jnp.int32)
        hi = hi + (start < r0 + LANES).astype(jnp.int32)
    return lo, hi


def _dispatch_kernel(cnt_s, off_s, base_s, h_ref, lpos_ref, xe_hbm, xc_ref, sem, *, cap):
    b = pl.program_id(0)
    nblk = pl.num_programs(0)
    slot = b % 2
    h = h_ref[...]

    def chunk(k, carry):
        r0 = pl.multiple_of(k * LANES, LANES)
        rho = r0 + lax.broadcasted_iota(jnp.int32, (LANES, ROUTE_BLOCK), 0)
        e_lo, e_hi = _chunk_experts(cnt_s, base_s, b, r0)

        def mark(e, hit):
            return jnp.where(lpos_ref[0, pl.ds(e, 1), :] == rho, 1.0, hit)

        hit = lax.fori_loop(e_lo, e_hi, mark, jnp.zeros((LANES, ROUTE_BLOCK), F32))
        onehot = hit.astype(BF16)
        for f in range(ROW_TILE[0]):
            xc_ref[slot, _feature_rows(r0, LANES, f), :] = jnp.dot(
                onehot, h[:, f * LANES:(f + 1) * LANES], preferred_element_type=F32)
        return carry

    lax.fori_loop(0, (_block_rows(cnt_s, base_s, b) + LANES - 1) // LANES, chunk, 0)

    def copies(blk, sl):
        def make(e, done, rows):
            src = base_s[blk * N_EXPERTS + e] + done
            dst = e * cap + off_s[blk * N_EXPERTS + e] + done
            return pltpu.make_async_copy(xc_ref.at[sl, _tile_rows(src, rows)],
                                         xe_hbm.at[_tile_rows(dst, rows)], sem.at[sl])
        return make

    _segment_copies(_start, cnt_s, b, copies(b, slot))

    @pl.when(b > 0)
    def _():
        _segment_copies(_wait, cnt_s, b - 1, copies(b - 1, 1 - slot))

    @pl.when(b == nblk - 1)
    def _():
        _segment_copies(_wait, cnt_s, b, copies(b, slot))


def dispatch(hn, lpos_t, cnt, off, base, cap):
    n = hn.shape[0]
    nb = n // ROUTE_BLOCK
    grid_spec = pltpu.PrefetchScalarGridSpec(
        num_scalar_prefetch=3,
        grid=(nb,),
        in_specs=[pl.BlockSpec((ROUTE_BLOCK, D_MODEL), lambda b, *_: (b, 0)),
                  pl.BlockSpec((1, N_EXPERTS, ROUTE_BLOCK), lambda b, *_: (b, 0, 0))],
        out_specs=pl.BlockSpec(memory_space=pl.ANY),
        scratch_shapes=[pltpu.VMEM((2, MAX_BLOCK_ROWS * ROW_TILE[0], LANES), F32),
                        pltpu.SemaphoreType.DMA((2,))],
    )
    return pl.pallas_call(
        functools.partial(_dispatch_kernel, cap=cap),
        grid_spec=grid_spec,
        out_shape=jax.ShapeDtypeStruct((N_EXPERTS * cap * ROW_TILE[0], LANES), F32),
        compiler_params=_cparams(("arbitrary",)),
        name="dispatch",
    )(cnt, off, base, hn, lpos_t)


def _ffn_kernel(xe_ref, wg_ref, wu_ref, wd_ref, ye_ref, xs_ref, *, hid_chunk):
    m = xs_ref.shape[0]
    nf = ROW_TILE[0]
    for f in range(nf):
        xs_ref[:, f * LANES:(f + 1) * LANES] = xe_ref[pl.ds(f, m, stride=nf), :].astype(BF16)
    x = xs_ref[...]
    acc = jnp.zeros((m, D_MODEL), F32)
    for c in range(D_EXPERT // hid_chunk):
        cols = slice(c * hid_chunk, (c + 1) * hid_chunk)
        g = jnp.dot(x, wg_ref[0, :, cols], preferred_element_type=F32)
        u = jnp.dot(x, wu_ref[0, :, cols], preferred_element_type=F32)
        hid = (g * jax.nn.sigmoid(g) * u).astype(BF16)
        acc = acc + jnp.dot(hid, wd_ref[0, cols, :], preferred_element_type=F32)
    for f in range(nf):
        ye_ref[pl.ds(f, m, stride=nf), :] = acc[:, f * LANES:(f + 1) * LANES]


def expert_ffn(xe, wg, wu, wd, cap, tm=512, hid_chunk=512):
    nf = ROW_TILE[0]
    tiles = cap // tm
    io_spec = pl.BlockSpec((tm * nf, LANES), lambda e, j: (e * tiles + j, 0))
    return pl.pallas_call(
        functools.partial(_ffn_kernel, hid_chunk=hid_chunk),
        grid=(N_EXPERTS, tiles),
        in_specs=[io_spec,
                  pl.BlockSpec((1, D_MODEL, D_EXPERT), lambda e, j: (e, 0, 0)),
                  pl.BlockSpec((1, D_MODEL, D_EXPERT), lambda e, j: (e, 0, 0)),
                  pl.BlockSpec((1, D_EXPERT, D_MODEL), lambda e, j: (e, 0, 0))],
        out_specs=io_spec,
        out_shape=jax.ShapeDtypeStruct(xe.shape, F32),
        scratch_shapes=[pltpu.VMEM((tm, D_MODEL), BF16)],
        compiler_params=_cparams(("arbitrary", "arbitrary")),
        name="expert_ffn",
    )(xe, wg, wu, wd)


def _combine_kernel(cnt_s, off_s, base_s, x_ref, lpos_ref, aff_ref, fg_ref, ye_hbm, o_ref,
                    yc_ref, sem, *, cap, final):
    b = pl.program_id(0)
    nblk = pl.num_programs(0)
    slot = b % 2

    def copies(blk, sl):
        def make(e, done, rows):
            src = e * cap + off_s[blk * N_EXPERTS + e] + done
            dst = base_s[blk * N_EXPERTS + e] + done
            return pltpu.make_async_copy(ye_hbm.at[_tile_rows(src, rows)],
                                         yc_ref.at[sl, _tile_rows(dst, rows)], sem.at[sl])
        return make

    @pl.when(b == 0)
    def _():
        yc_ref[...] = jnp.zeros(yc_ref.shape, F32)
        _segment_copies(_start, cnt_s, 0, copies(0, 0))

    @pl.when(b + 1 < nblk)
    def _():
        _segment_copies(_start, cnt_s, b + 1, copies(b + 1, 1 - slot))

    _segment_copies(_wait, cnt_s, b, copies(b, slot))

    o_ref[...] = x_ref[...]

    def chunk(k, carry):
        r0 = pl.multiple_of(k * LANES, LANES)
        rho = r0 + lax.broadcasted_iota(jnp.int32, (LANES, ROUTE_BLOCK), 0)
        e_lo, e_hi = _chunk_experts(cnt_s, base_s, b, r0)

        def weigh(e, wt):
            return jnp.where(lpos_ref[0, pl.ds(e, 1), :] == rho, aff_ref[0, pl.ds(e, 1), :], wt)

        wt = lax.fori_loop(e_lo, e_hi, weigh, jnp.zeros((LANES, ROUTE_BLOCK), F32))
        w = wt.T
        hi = w.astype(BF16)
        lo = (w - hi.astype(F32)).astype(BF16)
        for f in range(ROW_TILE[0]):
            y = yc_ref[slot, _feature_rows(r0, LANES, f), :].astype(BF16)
            o_ref[:, f * LANES:(f + 1) * LANES] += (
                jnp.dot(hi, y, preferred_element_type=F32)
                + jnp.dot(lo, y, preferred_element_type=F32))
        return carry

    lax.fori_loop(0, (_block_rows(cnt_s, base_s, b) + LANES - 1) // LANES, chunk, 0)
    if final:
        o_ref[...] = _rms(o_ref[...], fg_ref[...])


def combine(x1, lpos_t, aff_t, final_gain, ye, cnt, off, base, cap, final):
    n = x1.shape[0]
    nb = n // ROUTE_BLOCK
    by_block = pl.BlockSpec((1, N_EXPERTS, ROUTE_BLOCK), lambda b, *_: (b, 0, 0))
    grid_spec = pltpu.PrefetchScalarGridSpec(
        num_scalar_prefetch=3,
        grid=(nb,),
        in_specs=[pl.BlockSpec((ROUTE_BLOCK, D_MODEL), lambda b, *_: (b, 0)),
                  by_block, by_block,
                  pl.BlockSpec((1, D_MODEL), lambda b, *_: (0, 0)),
                  pl.BlockSpec(memory_space=pl.ANY)],
        out_specs=pl.BlockSpec((ROUTE_BLOCK, D_MODEL), lambda b, *_: (b, 0)),
        scratch_shapes=[pltpu.VMEM((2, MAX_BLOCK_ROWS * ROW_TILE[0], LANES), F32),
                        pltpu.SemaphoreType.DMA((2,))],
    )
    return pl.pallas_call(
        functools.partial(_combine_kernel, cap=cap, final=final),
        grid_spec=grid_spec,
        out_shape=jax.ShapeDtypeStruct((n, D_MODEL), F32),
        compiler_params=_cparams(("arbitrary",)),
        name="combine",
    )(cnt, off, base, x1, lpos_t, aff_t, final_gain.reshape(1, D_MODEL), ye)


def moe(x1, hn, aff, wg, wu, wd, final_gain, final):
    n = x1.shape[0]
    cap = EC_CAPACITY * n // N_EXPERTS
    lpos_t, aff_t, cnt, off, base = route(aff, cap)
    xe = dispatch(hn, lpos_t, cnt, off, base, cap)
    ye = expert_ffn(xe, wg, wu, wd, cap)
    return combine(x1, lpos_t, aff_t, final_gain, ye, cnt, off, base, cap, final)


def _trunk(x, p):
    batch, seq, _ = x.shape
    n = batch * seq
    x = x.reshape(n, D_MODEL)
    for l in range(DEPTH):
        a_arr, c_arr, *groups = in_proj(x, p["norm_mix"][l], p["w_in"][l], batch, seq)
        qx, k2, vx = mixer_a_prep(a_arr, p["q_gain"][l], p["k_gain"][l], batch, seq)
        oa = mixer_a(qx, k2, vx, batch, seq)
        obs = [mixer_b_group(groups[g], p["rel_bias"], g) for g in range(B_GROUPS)]
        oc = mixer_c(c_arr, p["na_rpb"][l], batch, seq)
        x1, hn, aff = out_proj(oa, obs, oc, x, p["out_gain"][l], p["w_out"][l],
                               p["norm_ffn"][l], p["w_router"][l], seq)
        x = moe(x1, hn, aff, p["w_gate"][l], p["w_up"][l], p["w_down"][l],
                p["final_norm"], final=(l == DEPTH - 1))
    return x.reshape(batch, seq, D_MODEL)


def kernel(x_prompt, x_sample, w_in, w_out, norm_mix, norm_ffn, q_gain, k_gain, out_gain,
           na_rpb, rel_bias, w_router, w_gate, w_up, w_down, final_norm):
    p = dict(w_in=permute_w_in(w_in).astype(BF16), w_out=w_out.astype(BF16), norm_mix=norm_mix,
             norm_ffn=norm_ffn, q_gain=q_gain, k_gain=k_gain, out_gain=out_gain,
             na_rpb=na_rpb, rel_bias=rel_bias, w_router=w_router,
             w_gate=w_gate.astype(BF16), w_up=w_up.astype(BF16), w_down=w_down.astype(BF16),
             final_norm=final_norm)
    return (_trunk(x_prompt, p), _trunk(x_sample, p))
```

```python
import functools
import math

import jax
import jax.numpy as jnp
import numpy as np
from jax import lax
from jax.experimental import pallas as pl
from jax.experimental.pallas import tpu as pltpu

D_MODEL = 1024
DEPTH = 2
GRID_W = 64
HEAD_DIM = 64
EPS = 1e-6
A_HEADS = 8
A_KV_HEADS = 2
A_GROUP = A_HEADS // A_KV_HEADS
ROPE_THETA = 10000.0
DILATED_PAIRS = ((128, 1), (512, 4), (2048, 16))
B_GROUPS = 3
B_HEADS = 4
C_HEADS = 4
NA_ROWS = 8
NA_COLS = 16
N_BUCKETS = 32
MAX_DISTANCE = 2048
N_EXPERTS = 16
EC_CAPACITY = 2
D_EXPERT = 2 * D_MODEL
A_Q = A_HEADS * HEAD_DIM
A_KV = A_KV_HEADS * HEAD_DIM
B_OUT = B_HEADS * HEAD_DIM
C_W = C_HEADS * HEAD_DIM
D_IN = 3840
D_MIX = A_Q + B_OUT + C_W

LANES = 128
PROJ_CHUNK = 768
NEG = -1e30
LOG2E = 1.4426950408889634
VMEM_LIMIT = 56 * 1024 * 1024

F32 = jnp.float32
BF16 = jnp.bfloat16


def _cparams(sem):
    return pltpu.CompilerParams(dimension_semantics=sem, vmem_limit_bytes=VMEM_LIMIT)


def permute_w_in(w_in):
    b0 = A_Q + 2 * A_KV
    bw = B_GROUPS * B_OUT
    parts = [w_in[..., :b0], w_in[..., b0 + 3 * bw:]]
    for g in range(B_GROUPS):
        parts += [w_in[..., b0 + s * bw + g * B_OUT:b0 + s * bw + (g + 1) * B_OUT] for s in range(3)]
    return jnp.concatenate(parts, axis=-1)


def _inproj_kernel(x_ref, g_ref, w_ref, a_ref, c_ref, *rest):
    g_refs, scr_ref = rest[:B_GROUPS], rest[B_GROUPS]
    x = x_ref[...]
    tm = x.shape[0]
    ms = jnp.mean(x * x, axis=-1, keepdims=True)
    h = (x * lax.rsqrt(ms + EPS) * g_ref[...]).astype(BF16)

    def proj(k):
        return jnp.dot(h, w_ref[:, k * PROJ_CHUNK:(k + 1) * PROJ_CHUNK], preferred_element_type=F32)

    a_ref[...] = proj(0).astype(BF16)
    c_ref[...] = proj(1).astype(BF16)
    for g, g_ref_out in enumerate(g_refs):
        d = DILATED_PAIRS[g][1]
        if d == 1:
            g_ref_out[0, 0] = proj(2 + g).astype(BF16)
        else:
            res = proj(2 + g)
            ntile = PROJ_CHUNK // LANES
            for c in range(ntile):
                scr_ref[c] = res[:, c * LANES:(c + 1) * LANES]
            for r in range(d):
                for c in range(ntile):
                    g_ref_out[0, r, :, c * LANES:(c + 1) * LANES] = (
                        scr_ref[c, pl.ds(r, tm // d, stride=d), :].astype(BF16))


def in_proj(x, gain, w_perm_bf16, batch, seq, tm=512):
    n = batch * seq
    nt = seq // tm
    row = pl.BlockSpec((tm, PROJ_CHUNK), lambda i: (i, 0))
    g_specs, g_shapes = [], []
    for _, d in DILATED_PAIRS:
        g_specs.append(pl.BlockSpec((1, d, tm // d, PROJ_CHUNK), lambda i: (i // nt, 0, i % nt, 0)))
        g_shapes.append(jax.ShapeDtypeStruct((batch, d, seq // d, PROJ_CHUNK), BF16))
    return pl.pallas_call(
        _inproj_kernel,
        grid=(n // tm,),
        in_specs=[pl.BlockSpec((tm, D_MODEL), lambda i: (i, 0)),
                  pl.BlockSpec((1, D_MODEL), lambda i: (0, 0)),
                  pl.BlockSpec((D_MODEL, D_IN), lambda i: (0, 0))],
        out_specs=[row, row] + g_specs,
        out_shape=[jax.ShapeDtypeStruct((n, PROJ_CHUNK), BF16)] * 2 + g_shapes,
        scratch_shapes=[pltpu.VMEM((PROJ_CHUNK // LANES, tm, LANES), F32)],
        compiler_params=_cparams(("arbitrary",)),
        name="in_proj",
    )(x, gain.reshape(1, D_MODEL), w_perm_bf16)


def _rope_tables(seq):
    t = np.arange(seq)
    row = (t // GRID_W).astype(np.float64)
    col = (t % GRID_W).astype(np.float64)
    half = HEAD_DIM // 2
    freqs = ROPE_THETA ** (-np.arange(0, half, 2, dtype=np.float64) / half)
    ang = np.concatenate([row[:, None] * freqs, col[:, None] * freqs], axis=-1)
    cos = np.repeat(np.cos(ang), 2, axis=-1)
    sin = np.repeat(np.sin(ang), 2, axis=-1)
    sign = np.tile(np.array([-1.0, 1.0], np.float32), HEAD_DIM // 2)
    cos2 = np.tile(cos, (1, 2)).astype(np.float32)
    sin2 = np.tile(sin * sign, (1, 2)).astype(np.float32)
    return jnp.asarray(cos2), jnp.asarray(sin2)


def _headnorm_rope(x, gain2, cos, sin, seg):
    sq = x * x
    hi = sq.astype(BF16)
    lo = (sq - hi.astype(F32)).astype(BF16)
    ms = (jnp.dot(hi, seg, preferred_element_type=F32)
          + jnp.dot(lo, seg, preferred_element_type=F32))
    y = x * lax.rsqrt(ms + EPS) * gain2
    lane = lax.broadcasted_iota(jnp.int32, y.shape, 1)
    even = (lane % 2) == 0
    swapped = jnp.where(even, pltpu.roll(y, LANES - 1, 1), pltpu.roll(y, 1, 1))
    return y * cos + swapped * sin


def _aprep_kernel(a_ref, cos_ref, sin_ref, qg_ref, kg_ref, q_ref, k_ref, v_ref):
    cos = cos_ref[...]
    sin = sin_ref[...]
    r = lax.broadcasted_iota(jnp.int32, (LANES, LANES), 0) // HEAD_DIM
    c = lax.broadcasted_iota(jnp.int32, (LANES, LANES), 1) // HEAD_DIM
    seg = jnp.where(r == c, 1.0 / HEAD_DIM, 0.0).astype(BF16)
    tm = a_ref.shape[0]
    lane = lax.broadcasted_iota(jnp.int32, (tm, LANES), 1)
    low = lane < HEAD_DIM
    qscale = (HEAD_DIM ** -0.5) * LOG2E
    for t in range(A_Q // LANES):
        x = a_ref[:, t * LANES:(t + 1) * LANES].astype(F32)
        y = _headnorm_rope(x, qg_ref[...], cos, sin, seg) * qscale
        swapped = pltpu.roll(y, HEAD_DIM, 1)
        if (2 * t) // A_GROUP == 0:
            q_even, q_odd = jnp.where(low, y, 0.0), jnp.where(low, swapped, 0.0)
        else:
            q_even, q_odd = jnp.where(low, 0.0, swapped), jnp.where(low, 0.0, y)
        q_ref[0, 2 * t] = q_even.astype(BF16)
        q_ref[0, 2 * t + 1] = q_odd.astype(BF16)
    xk = a_ref[:, A_Q:A_Q + LANES].astype(F32)
    k_ref[0] = _headnorm_rope(xk, kg_ref[...], cos, sin, seg).astype(BF16)
    xv = a_ref[:, A_Q + LANES:A_Q + 2 * LANES].astype(F32)
    one_col = (lane == HEAD_DIM).astype(F32)
    v_ref[0, 0] = (jnp.where(low, xv, 0.0) + one_col).astype(BF16)
    v_ref[0, 1] = (jnp.where(low, pltpu.roll(xv, HEAD_DIM, 1), 0.0) + one_col).astype(BF16)


def mixer_a_prep(a_arr, q_gain, k_gain, batch, seq, tm=512):
    cos2, sin2 = _rope_tables(seq)
    nb = seq // tm
    qg = jnp.tile(q_gain.astype(F32), 2).reshape(1, LANES)
    kg = jnp.tile(k_gain.astype(F32), 2).reshape(1, LANES)
    return pl.pallas_call(
        _aprep_kernel,
        grid=(batch, nb),
        in_specs=[pl.BlockSpec((tm, PROJ_CHUNK), lambda b, i: (b * nb + i, 0)),
                  pl.BlockSpec((tm, LANES), lambda b, i: (i, 0)),
                  pl.BlockSpec((tm, LANES), lambda b, i: (i, 0)),
                  pl.BlockSpec((1, LANES), lambda b, i: (0, 0)),
                  pl.BlockSpec((1, LANES), lambda b, i: (0, 0))],
        out_specs=[pl.BlockSpec((1, A_HEADS, tm, LANES), lambda b, i: (b, 0, i, 0)),
                   pl.BlockSpec((1, tm, LANES), lambda b, i: (b, i, 0)),
                   pl.BlockSpec((1, A_KV_HEADS, tm, LANES), lambda b, i: (b, 0, i, 0))],
        out_shape=[jax.ShapeDtypeStruct((batch, A_HEADS, seq, LANES), BF16),
                   jax.ShapeDtypeStruct((batch, seq, LANES), BF16),
                   jax.ShapeDtypeStruct((batch, A_KV_HEADS, seq, LANES), BF16)],
        compiler_params=_cparams(("arbitrary", "arbitrary")),
        name="mixer_a_prep",
    )(a_arr, cos2, sin2, qg, kg)


FLASH_ROWS = 512


def _flash_kernel(q_ref, k_ref, v_ref, o_ref, m_ref, acc_ref):
    j = pl.program_id(3)
    tq = q_ref.shape[2]
    tk = k_ref.shape[1]

    @pl.when(j == 0)
    def _():
        m_ref[...] = jnp.full(m_ref.shape, -jnp.inf, F32)
        acc_ref[...] = jnp.zeros(acc_ref.shape, F32)

    k = k_ref[0]
    v = v_ref[0, 0]
    for h in range(A_GROUP):
        for r0 in range(0, tq, FLASH_ROWS):
            rows = pl.ds(r0, FLASH_ROWS)
            s = lax.dot_general(q_ref[0, h, rows, :], k, (((1,), (1,)), ((), ())),
                                preferred_element_type=F32)
            m_prev = m_ref[h, rows, :]
            m_new = jnp.maximum(m_prev, jnp.max(s, axis=-1, keepdims=True))
            alpha = jnp.exp2(m_prev - m_new)
            p = jnp.exp2(s - jnp.tile(m_new, (1, tk // LANES))).astype(BF16)
            acc_ref[h, rows, :] = alpha * acc_ref[h, rows, :] + jnp.dot(
                p, v, preferred_element_type=F32)
            m_ref[h, rows, :] = m_new

    @pl.when(j == pl.num_programs(3) - 1)
    def _():
        lane = lax.broadcasted_iota(jnp.int32, (tq, LANES), 1)
        low = lane < HEAD_DIM

        def normed(h):
            acc = acc_ref[h]
            return acc / acc[:, HEAD_DIM:HEAD_DIM + 1]

        for t in range(A_GROUP // 2):
            o_ref[:, t * LANES:(t + 1) * LANES] = jnp.where(
                low, normed(2 * t), pltpu.roll(normed(2 * t + 1), HEAD_DIM, 1)).astype(o_ref.dtype)


def mixer_a(qx, k2, vx, batch, seq, tq=1024, tk=2048):
    nq = seq // tq
    grp_w = A_GROUP * HEAD_DIM
    return pl.pallas_call(
        _flash_kernel,
        grid=(batch, A_KV_HEADS, nq, seq // tk),
        in_specs=[pl.BlockSpec((1, A_GROUP, tq, LANES), lambda b, h, i, j: (b, h, i, 0)),
                  pl.BlockSpec((1, tk, LANES), lambda b, h, i, j: (b, j, 0)),
                  pl.BlockSpec((1, 1, tk, LANES), lambda b, h, i, j: (b, h, j, 0))],
        out_specs=pl.BlockSpec((tq, grp_w), lambda b, h, i, j: (b * nq + i, h)),
        out_shape=jax.ShapeDtypeStruct((batch * seq, A_Q), F32),
        scratch_shapes=[pltpu.VMEM((A_GROUP, tq, LANES), F32),
                        pltpu.VMEM((A_GROUP, tq, LANES), F32)],
        compiler_params=_cparams(("arbitrary",) * 4),
        name="mixer_a_flash",
    )(qx, k2, vx)


B_TL = 128
B_CHAINS = 8


def _toeplitz(vec, nrow, ncol, shift):
    period = vec.shape[-1] + 1
    lead = vec.shape[:-1]
    padded = jnp.concatenate([vec, jnp.zeros(lead + (1,), vec.dtype)], axis=-1)
    flat = jnp.tile(padded, (1,) * len(lead) + (nrow,))[..., :nrow * (period - 1)]
    skew = flat.reshape(lead + (nrow, period - 1))
    return skew[..., shift:shift + ncol]
def _t5_bucket(rel):
    nb = N_BUCKETS // 2
    max_exact = nb // 2
    ret = (rel > 0).astype(jnp.int32) * nb
    n = jnp.abs(rel)
    large = max_exact + (jnp.log(jnp.maximum(n, 1).astype(jnp.float32) / max_exact)
                         / math.log(MAX_DISTANCE / max_exact) * (nb - max_exact)).astype(jnp.int32)
    large = jnp.minimum(large, nb - 1)
    return ret + jnp.where(n < max_exact, n, large)


def _dilated_bias(rel_bias_g, dilation, tl, radius):
    off = np.arange(2 * tl + 2 * radius - 1) - (tl - 1) - radius
    per_off = rel_bias_g[_t5_bucket(jnp.asarray(off * dilation))].astype(F32).T
    per_off = jnp.where(jnp.asarray(np.abs(off) <= radius)[None], per_off, NEG)
    return _toeplitz(per_off, tl, tl + 2 * radius, tl - 1)


def _head_mask(shape, h):
    lane = lax.broadcasted_iota(jnp.int32, shape, 1)
    return (lane < HEAD_DIM) if h % 2 == 0 else (lane >= HEAD_DIM)


def _windowed_heads(q, kwin, vwin, bias_ref, valid, want_lse):
    tl = q.shape[0]
    scale = jnp.asarray(HEAD_DIM ** -0.5, q.dtype)
    sel0 = _head_mask((tl, LANES), 0)
    o_tiles, lse_tiles = [], []
    for t in range(2):
        qt = q[:, t * LANES:(t + 1) * LANES] * scale
        kt = kwin[:, t * LANES:(t + 1) * LANES]
        vt = vwin[:, t * LANES:(t + 1) * LANES]
        zero = jnp.zeros_like(qt)
        q2 = jnp.concatenate([jnp.where(sel0, qt, zero), jnp.where(sel0, zero, qt)], axis=0)
        s = lax.dot_general(q2, kt, (((1,), (1,)), ((), ())), preferred_element_type=F32)
        s = s + jnp.concatenate([bias_ref[2 * t], bias_ref[2 * t + 1]], axis=0)
        if valid is not None:
            s = jnp.where(valid, s, NEG)
        m = jnp.max(s, axis=-1, keepdims=True)
        p = jnp.exp(s - m)
        den = jnp.sum(p, axis=-1, keepdims=True)
        o2 = jnp.dot(p.astype(BF16), vt, preferred_element_type=F32) / den
        o_tiles.append(jnp.where(sel0, o2[:tl], o2[tl:]))
        if want_lse:
            lse2 = m + jnp.log(den)
            lse_tiles.append(jnp.where(sel0, lse2[:tl], lse2[tl:]))
    return o_tiles, lse_tiles


def _dilated_kernel(q_ref, kp_ref, km_ref, kn_ref, vp_ref, vm_ref, vn_ref, bias_ref,
                    o_ref, lse_ref, *, sub_len, radius):
    i = pl.program_id(2)
    nres, tls = q_ref.shape[0], q_ref.shape[1]
    nsub = tls // B_TL
    win = B_TL + 2 * radius
    for r in range(nres):
        kwin = jnp.concatenate([kp_ref[r], km_ref[r], kn_ref[r]], axis=0)
        vwin = jnp.concatenate([vp_ref[r], vm_ref[r], vn_ref[r]], axis=0)
        for j in range(nsub):
            valid = None
            if j == 0 or j == nsub - 1:
                kpos = (i * tls + j * B_TL - radius
                        + lax.broadcasted_iota(jnp.int32, (1, win), 1))
                valid = (kpos >= 0) & (kpos < sub_len)
            rows = slice(j * B_TL, (j + 1) * B_TL)
            o_tiles, lse_tiles = _windowed_heads(
                q_ref[r, rows, :], kwin[j * B_TL:j * B_TL + win], vwin[j * B_TL:j * B_TL + win],
                bias_ref, valid, True)
            for t in range(2):
                o_ref[r, rows, t * LANES:(t + 1) * LANES] = o_tiles[t]
                lse_ref[r, rows, t * LANES:(t + 1) * LANES] = lse_tiles[t]


def mixer_b_group(gqkv, rel_bias, g):
    batch, d, sub_len, _ = gqkv.shape
    window = DILATED_PAIRS[g][0]
    radius = window // (2 * d)
    assert sub_len % B_TL == 0 and B_TL % radius == 0
    nsub = min(B_CHAINS, sub_len // B_TL)
    nres = min(d, B_CHAINS // nsub)
    tls = nsub * B_TL
    hpb = tls // radius
    nhalo = sub_len // radius
    bias = _dilated_bias(rel_bias[:, g * B_HEADS:(g + 1) * B_HEADS], d, B_TL, radius)

    def main(col):
        return pl.BlockSpec((None, nres, tls, B_OUT), lambda b, r, i: (b, r, i, col))

    def prev(col):
        return pl.BlockSpec((None, nres, radius, B_OUT),
                            lambda b, r, i: (b, r, jnp.maximum(i * hpb - 1, 0), col))

    def nxt(col):
        return pl.BlockSpec((None, nres, radius, B_OUT),
                            lambda b, r, i: (b, r, jnp.minimum((i + 1) * hpb, nhalo - 1), col))

    return pl.pallas_call(
        functools.partial(_dilated_kernel, sub_len=sub_len, radius=radius),
        grid=(batch, d // nres, sub_len // tls),
        in_specs=[main(0), prev(1), main(1), nxt(1), prev(2), main(2), nxt(2),
                  pl.BlockSpec((B_HEADS, B_TL, B_TL + 2 * radius), lambda b, r, i: (0, 0, 0))],
        out_specs=[main(0), main(0)],
        out_shape=[jax.ShapeDtypeStruct((batch, d, sub_len, B_OUT), F32)] * 2,
        compiler_params=_cparams(("arbitrary",) * 3),
        name=f"mixer_b_g{g}",
    )(gqkv, gqkv, gqkv, gqkv, gqkv, gqkv, gqkv, bias)


C_QROWS = 4
C_TQ = C_QROWS * GRID_W


def _na_bias(rpb):
    u = np.arange(C_TQ)[:, None] // GRID_W
    cq = np.arange(C_TQ)[:, None] % GRID_W
    w = np.arange(3 * C_TQ)[None, :] // GRID_W
    ck = np.arange(3 * C_TQ)[None, :] % GRID_W
    cs = np.clip(cq - NA_COLS // 2, 0, GRID_W - NA_COLS)
    col_ok = (ck >= cs) & (ck < cs + NA_COLS)
    row_ok = [(w >= C_QROWS) & (w < C_QROWS + NA_ROWS) & (u >= 0),
              (w >= u) & (w < u + NA_ROWS),
              (w >= 0) & (w < NA_ROWS) & (u >= 0)]
    side = GRID_W - NA_COLS
    padded = jnp.pad(rpb.astype(F32), ((0, 0), (0, 0), (side, side)))
    blocks = _toeplitz(padded, GRID_W, GRID_W, GRID_W - 1)
    drow = np.clip((np.arange(3 * C_QROWS)[None, :] - C_QROWS) - np.arange(C_QROWS)[:, None]
                   + NA_ROWS - 1, 0, 2 * NA_ROWS - 2)
    tiles = jnp.take(blocks, jnp.asarray(drow.reshape(-1)), axis=1)
    tiles = tiles.reshape(C_HEADS, C_QROWS, 3 * C_QROWS, GRID_W, GRID_W)
    vals = tiles.transpose(0, 1, 3, 2, 4).reshape(C_HEADS, C_TQ, 3 * C_TQ)
    return jnp.stack([jnp.where(jnp.asarray(r & col_ok)[None], vals, NEG) for r in row_ok])


C_PAIR = 2


def _na_kernel(q_ref, kp_ref, kc_ref, kn_ref, vp_ref, vc_ref, vn_ref, *rest):
    bias_refs, o_ref = rest[:C_PAIR], rest[C_PAIR]
    kall = jnp.concatenate([kp_ref[...], kc_ref[...], kn_ref[...]], axis=0)
    vall = jnp.concatenate([vp_ref[...], vc_ref[...], vn_ref[...]], axis=0)
    for j in range(C_PAIR):
        rows = slice(j * C_TQ, (j + 1) * C_TQ)
        win = slice(j * C_TQ, (j + 3) * C_TQ)
        o_tiles, _ = _windowed_heads(q_ref[rows, :], kall[win], vall[win], bias_refs[j].at[0],
                                     None, False)
        for t in range(2):
            o_ref[rows, t * LANES:(t + 1) * LANES] = o_tiles[t]


def mixer_c(c_arr, rpb, batch, seq):
    rows = seq // GRID_W
    assert rows % (C_PAIR * C_QROWS) == 0 and rows >= 3 * C_QROWS
    nblk = rows // C_QROWS
    nstep = nblk // C_PAIR
    n = batch * seq
    bias = _na_bias(rpb)

    def halo(col, blk_of_step):
        def imap(b, i):
            return (b * nblk + jnp.clip(blk_of_step(i), 0, nblk - 1), col)
        return pl.BlockSpec((C_TQ, C_W), imap)

    def main(col):
        return pl.BlockSpec((C_PAIR * C_TQ, C_W), lambda b, i: (b * nstep + i, col))

    def bias_spec(j):
        def imap(b, i):
            blk = i * C_PAIR + j
            return (jnp.where(blk == 0, 0, jnp.where(blk == nblk - 1, 2, 1)), 0, 0, 0)
        return pl.BlockSpec((1, C_HEADS, C_TQ, 3 * C_TQ), imap)

    before = lambda i: i * C_PAIR - 1
    after = lambda i: (i + 1) * C_PAIR
    return pl.pallas_call(
        _na_kernel,
        grid=(batch, nstep),
        in_specs=[main(0), halo(1, before), main(1), halo(1, after),
                  halo(2, before), main(2), halo(2, after)]
                 + [bias_spec(j) for j in range(C_PAIR)],
        out_specs=main(0),
        out_shape=jax.ShapeDtypeStruct((n, C_W), F32),
        compiler_params=_cparams(("arbitrary", "arbitrary")),
        name="mixer_c",
    )(c_arr, c_arr, c_arr, c_arr, c_arr, c_arr, c_arr, *([bias] * C_PAIR))


def _rms(x, gain):
    ms = jnp.mean(x * x, axis=-1, keepdims=True)
    return x * lax.rsqrt(ms + EPS) * gain


def _token_order(src_ref, scr_ref):
    d, sub = src_ref.shape[1], src_ref.shape[2]
    if d == 1:
        return src_ref[0, 0]
    ntile = scr_ref.shape[0]
    for r in range(d):
        for c in range(ntile):
            scr_ref[c, pl.ds(r, sub, stride=d), :] = src_ref[0, r, :, c * LANES:(c + 1) * LANES]
    return jnp.concatenate([scr_ref[c] for c in range(ntile)], axis=1)


def _outproj_kernel(oa_ref, ob0_ref, ob1_ref, ob2_ref, l0_ref, l1_ref, l2_ref, oc_ref, x_ref,
                    og_ref, w_ref, nf_ref, wr_ref, x1_ref, hn_ref, aff_ref, *scr):
    ob0, ob1, ob2 = (_token_order(r, s) for r, s in zip((ob0_ref, ob1_ref, ob2_ref), scr[:3]))
    l0, l1, l2 = (_token_order(r, s) for r, s in zip((l0_ref, l1_ref, l2_ref), scr[3:]))
    m = jnp.maximum(jnp.maximum(l0, l1), l2)
    e0, e1, e2 = jnp.exp(l0 - m), jnp.exp(l1 - m), jnp.exp(l2 - m)
    ob = (e0 * ob0 + e1 * ob1 + e2 * ob2) / (e0 + e1 + e2)
    og = og_ref[...]
    na = _rms(oa_ref[...], og[:, :A_Q]).astype(BF16)
    nb = _rms(ob, og[:, A_Q:A_Q + B_OUT]).astype(BF16)
    nc = _rms(oc_ref[...], og[:, A_Q + B_OUT:]).astype(BF16)
    mix = (jnp.dot(na, w_ref[:A_Q], preferred_element_type=F32)
           + jnp.dot(nb, w_ref[A_Q:A_Q + B_OUT], preferred_element_type=F32)
           + jnp.dot(nc, w_ref[A_Q + B_OUT:], preferred_element_type=F32))
    x1 = x_ref[...] + mix
    x1_ref[...] = x1
    hn = _rms(x1, nf_ref[...])
    hn_ref[...] = hn.astype(BF16)
    wr = wr_ref[...]
    h_hi, w_hi = hn.astype(BF16), wr.astype(BF16)
    h_lo = (hn - h_hi.astype(F32)).astype(BF16)
    w_lo = (wr - w_hi.astype(F32)).astype(BF16)
    logits = (jnp.dot(h_hi, w_hi, preferred_element_type=F32)
              + jnp.dot(h_lo, w_hi, preferred_element_type=F32)
              + jnp.dot(h_hi, w_lo, preferred_element_type=F32))
    p = jnp.exp(logits - jnp.max(logits, axis=-1, keepdims=True))
    aff_ref[...] = p / jnp.sum(p, axis=-1, keepdims=True)


def out_proj(oa, obs, oc, x, out_gain, w_out_bf16, norm_ffn, w_router, seq, tm=512):
    n = x.shape[0]
    nt = seq // tm
    row = lambda w: pl.BlockSpec((tm, w), lambda i: (i, 0))
    full = lambda a: pl.BlockSpec(a.shape, lambda i: (0,) * a.ndim)
    sub = [pl.BlockSpec((1, d, tm // d, B_OUT), lambda i: (i // nt, 0, i % nt, 0))
           for _, d in DILATED_PAIRS]
    og = out_gain.reshape(1, D_MIX)
    nf = norm_ffn.reshape(1, D_MODEL)
    return pl.pallas_call(
        _outproj_kernel,
        grid=(n // tm,),
        in_specs=[row(A_Q)] + sub + sub + [row(C_W), row(D_MODEL),
                  full(og), full(w_out_bf16), full(nf), full(w_router)],
        out_specs=[row(D_MODEL), row(D_MODEL), row(N_EXPERTS)],
        out_shape=[jax.ShapeDtypeStruct((n, D_MODEL), F32),
                   jax.ShapeDtypeStruct((n, D_MODEL), BF16),
                   jax.ShapeDtypeStruct((n, N_EXPERTS), F32)],
        scratch_shapes=[pltpu.VMEM((B_OUT // LANES, tm, LANES), F32)] * (2 * B_GROUPS),
        compiler_params=_cparams(("arbitrary",)),
        name="out_proj",
    )(oa, obs[0][0], obs[1][0], obs[2][0], obs[0][1], obs[1][1], obs[2][1], oc, x,
      og, w_out_bf16, nf, w_router)


ROUTE_BLOCK = 2 * LANES
ROUTE_ROWS = ROUTE_BLOCK // LANES


def _route_kernel(aff_ref, lpos_ref, cnt_ref, off_ref, base_ref, base_acc, *, cap):
    e = pl.program_id(0)

    @pl.when(e == 0)
    def _():
        base_acc[...] = jnp.zeros(base_acc.shape, F32)

    bits = lax.bitcast_convert_type(aff_ref[0], jnp.int32)
    nb = bits.shape[0]

    def search(i, thr):
        cand = thr | jnp.left_shift(jnp.int32(1), 30 - i)
        cnt = jnp.sum((bits >= cand).astype(F32), keepdims=True)
        return jnp.where(cnt >= cap, cand, thr)

    thr = lax.fori_loop(0, 31, search, jnp.zeros((1, 1), jnp.int32))
    gt = bits > thr
    eq = bits == thr
    need = cap - jnp.sum(gt.astype(F32), keepdims=True)

    ri = lax.broadcasted_iota(jnp.int32, (LANES, LANES), 0)
    ci = lax.broadcasted_iota(jnp.int32, (LANES, LANES), 1)
    upper = (ri <= ci).astype(F32).astype(BF16)
    ones = jnp.ones((LANES, LANES), BF16)
    rr = lax.broadcasted_iota(jnp.int32, (nb, nb), 0)
    rc = lax.broadcasted_iota(jnp.int32, (nb, nb), 1)
    strict_lower = (rc < rr).astype(F32).astype(BF16)
    same_block = (rc // ROUTE_ROWS) == (rr // ROUTE_ROWS)
    block_all = same_block.astype(F32).astype(BF16)
    block_lower = (same_block & (rc < rr)).astype(F32).astype(BF16)

    def prefix(mask_f32):
        mb = mask_f32.astype(BF16)
        in_row = jnp.dot(mb, upper, preferred_element_type=F32)
        tot = jnp.dot(mb, ones, preferred_element_type=F32)
        offs = jnp.dot(strict_lower, tot.astype(BF16), preferred_element_type=F32)
        return in_row, tot, offs

    eq_f = eq.astype(F32)
    in_row_e, _, offs_e = prefix(eq_f)
    rank_eq = offs_e + in_row_e - eq_f
    sel = gt | (eq & (rank_eq < need))
    in_row, tot, offs = prefix(sel.astype(F32))
    tot_b = tot.astype(BF16)
    in_block = jnp.dot(block_lower, tot_b, preferred_element_type=F32)
    block_tot = jnp.dot(block_all, tot_b, preferred_element_type=F32)
    base = base_acc[...]
    lpos_ref[0] = jnp.where(sel, base + in_block + in_row - 1.0, -1.0).astype(jnp.int32)
    cnt_ref[0] = block_tot.astype(jnp.int32)
    off_ref[0] = (offs - in_block).astype(jnp.int32)
    base_ref[0] = base.astype(jnp.int32)
    base_acc[...] = base + block_tot


def route(aff, cap):
    n = aff.shape[0]
    nb = n // LANES
    aff_t = aff.T.reshape(N_EXPERTS, nb, LANES)
    spec = pl.BlockSpec((1, nb, LANES), lambda e: (e, 0, 0))
    outs = pl.pallas_call(
        functools.partial(_route_kernel, cap=cap),
        grid=(N_EXPERTS,),
        in_specs=[spec],
        out_specs=[spec] * 4,
        out_shape=[jax.ShapeDtypeStruct((N_EXPERTS, nb, LANES), jnp.int32)] * 4,
        scratch_shapes=[pltpu.VMEM((nb, LANES), F32)],
        compiler_params=_cparams(("arbitrary",)),
        name="route",
    )(aff_t)
    lpos, cnt, off, base = outs
    nblk = n // ROUTE_BLOCK
    scal = lambda a: a[:, ::ROUTE_ROWS, 0].T.reshape(-1)
    by_block = lambda a: a.reshape(N_EXPERTS, nblk, ROUTE_BLOCK).transpose(1, 0, 2)
    return by_block(lpos), by_block(aff_t), scal(cnt), scal(off), scal(base)


ROW_TILE = (D_MODEL // LANES, LANES)


def _tile_rows(start, rows):
    nf = ROW_TILE[0]
    return pl.ds(pl.multiple_of(start * nf, nf), rows * nf)


def _feature_rows(start, rows, f):
    nf = ROW_TILE[0]
    return pl.ds(start * nf + f, rows, stride=nf)
SEG_PIECE = 16
SEG_BITS = tuple(1 << k for k in range(3, -1, -1))
MAX_BLOCK_ROWS = N_EXPERTS * ROUTE_BLOCK


def _segment_copies(fn, cnt_s, blk, make):
    for e in range(N_EXPERTS):
        c = cnt_s[blk * N_EXPERTS + e]
        npiece = c // SEG_PIECE

        def piece(i, carry):
            fn(make(e, i * SEG_PIECE, SEG_PIECE))
            return carry

        lax.fori_loop(0, npiece, piece, 0)
        rem = c - npiece * SEG_PIECE
        for bit in SEG_BITS:
            done = npiece * SEG_PIECE + (rem // (2 * bit)) * (2 * bit)

            @pl.when((rem & bit) != 0)
            def _():
                fn(make(e, done, bit))


def _start(cp):
    cp.start()


def _wait(cp):
    cp.wait()


def _block_rows(cnt_s, base_s, blk):
    last = blk * N_EXPERTS + N_EXPERTS - 1
    return base_s[last] + cnt_s[last]


def _chunk_experts(cnt_s, base_s, blk, r0):
    lo = jnp.int32(0)
    hi = jnp.int32(0)
    for e in range(N_EXPERTS):
        start = base_s[blk * N_EXPERTS + e]
        end = start + cnt_s[blk * N_EXPERTS + e]
        lo = lo + (end <= r0).astype(jnp.int32)
        hi = hi + (start < r0 + LANES).astype(jnp.int32)
    return lo, hi


def _dispatch_kernel(cnt_s, off_s, base_s, h_ref, lpos_ref, xe_hbm, xc_ref, sem, *, cap):
    b = pl.program_id(0)
    nblk = pl.num_programs(0)
    slot = b % 2
    h = h_ref[...]

    def chunk(k, carry):
        r0 = pl.multiple_of(k * LANES, LANES)
        rho = r0 + lax.broadcasted_iota(jnp.int32, (LANES, ROUTE_BLOCK), 0)
        e_lo, e_hi = _chunk_experts(cnt_s, base_s, b, r0)

        def mark(e, hit):
            return jnp.where(lpos_ref[0, pl.ds(e, 1), :] == rho, 1.0, hit)

        hit = lax.fori_loop(e_lo, e_hi, mark, jnp.zeros((LANES, ROUTE_BLOCK), F32))
        onehot = hit.astype(BF16)
        for f in range(ROW_TILE[0]):
            xc_ref[slot, _feature_rows(r0, LANES, f), :] = jnp.dot(
                onehot, h[:, f * LANES:(f + 1) * LANES], preferred_element_type=F32)
        return carry

    lax.fori_loop(0, (_block_rows(cnt_s, base_s, b) + LANES - 1) // LANES, chunk, 0)

    def copies(blk, sl):
        def make(e, done, rows):
            src = base_s[blk * N_EXPERTS + e] + done
            dst = e * cap + off_s[blk * N_EXPERTS + e] + done
            return pltpu.make_async_copy(xc_ref.at[sl, _tile_rows(src, rows)],
                                         xe_hbm.at[_tile_rows(dst, rows)], sem.at[sl])
        return make

    _segment_copies(_start, cnt_s, b, copies(b, slot))

    @pl.when(b > 0)
    def _():
        _segment_copies(_wait, cnt_s, b - 1, copies(b - 1, 1 - slot))

    @pl.when(b == nblk - 1)
    def _():
        _segment_copies(_wait, cnt_s, b, copies(b, slot))


def dispatch(hn, lpos_t, cnt, off, base, cap):
    n = hn.shape[0]
    nb = n // ROUTE_BLOCK
    grid_spec = pltpu.PrefetchScalarGridSpec(
        num_scalar_prefetch=3,
        grid=(nb,),
        in_specs=[pl.BlockSpec((ROUTE_BLOCK, D_MODEL), lambda b, *_: (b, 0)),
                  pl.BlockSpec((1, N_EXPERTS, ROUTE_BLOCK), lambda b, *_: (b, 0, 0))],
        out_specs=pl.BlockSpec(memory_space=pl.ANY),
        scratch_shapes=[pltpu.VMEM((2, MAX_BLOCK_ROWS * ROW_TILE[0], LANES), F32),
                        pltpu.SemaphoreType.DMA((2,))],
    )
    return pl.pallas_call(
        functools.partial(_dispatch_kernel, cap=cap),
        grid_spec=grid_spec,
        out_shape=jax.ShapeDtypeStruct((N_EXPERTS * cap * ROW_TILE[0], LANES), F32),
        compiler_params=_cparams(("arbitrary",)),
        name="dispatch",
    )(cnt, off, base, hn, lpos_t)


def _ffn_kernel(xe_ref, wg_ref, wu_ref, wd_ref, ye_ref, xs_ref, *, hid_chunk):
    m = xs_ref.shape[0]
    nf = ROW_TILE[0]
    for f in range(nf):
        xs_ref[:, f * LANES:(f + 1) * LANES] = xe_ref[pl.ds(f, m, stride=nf), :].astype(BF16)
    x = xs_ref[...]
    acc = jnp.zeros((m, D_MODEL), F32)
    for c in range(D_EXPERT // hid_chunk):
        cols = slice(c * hid_chunk, (c + 1) * hid_chunk)
        g = jnp.dot(x, wg_ref[0, :, cols], preferred_element_type=F32)
        u = jnp.dot(x, wu_ref[0, :, cols], preferred_element_type=F32)
        hid = (g * jax.nn.sigmoid(g) * u).astype(BF16)
        acc = acc + jnp.dot(hid, wd_ref[0, cols, :], preferred_element_type=F32)
    for f in range(nf):
        ye_ref[pl.ds(f, m, stride=nf), :] = acc[:, f * LANES:(f + 1) * LANES]


def expert_ffn(xe, wg, wu, wd, cap, tm=512, hid_chunk=512):
    nf = ROW_TILE[0]
    tiles = cap // tm
    io_spec = pl.BlockSpec((tm * nf, LANES), lambda e, j: (e * tiles + j, 0))
    return pl.pallas_call(
        functools.partial(_ffn_kernel, hid_chunk=hid_chunk),
        grid=(N_EXPERTS, tiles),
        in_specs=[io_spec,
                  pl.BlockSpec((1, D_MODEL, D_EXPERT), lambda e, j: (e, 0, 0)),
                  pl.BlockSpec((1, D_MODEL, D_EXPERT), lambda e, j: (e, 0, 0)),
                  pl.BlockSpec((1, D_EXPERT, D_MODEL), lambda e, j: (e, 0, 0))],
        out_specs=io_spec,
        out_shape=jax.ShapeDtypeStruct(xe.shape, F32),
        scratch_shapes=[pltpu.VMEM((tm, D_MODEL), BF16)],
        compiler_params=_cparams(("arbitrary", "arbitrary")),
        name="expert_ffn",
    )(xe, wg, wu, wd)


def _combine_kernel(cnt_s, off_s, base_s, x_ref, lpos_ref, aff_ref, fg_ref, ye_hbm, o_ref,
                    yc_ref, sem, *, cap, final):
    b = pl.program_id(0)
    nblk = pl.num_programs(0)
    slot = b % 2

    def copies(blk, sl):
        def make(e, done, rows):
            src = e * cap + off_s[blk * N_EXPERTS + e] + done
            dst = base_s[blk * N_EXPERTS + e] + done
            return pltpu.make_async_copy(ye_hbm.at[_tile_rows(src, rows)],
                                         yc_ref.at[sl, _tile_rows(dst, rows)], sem.at[sl])
        return make

    @pl.when(b == 0)
    def _():
        yc_ref[...] = jnp.zeros(yc_ref.shape, F32)
        _segment_copies(_start, cnt_s, 0, copies(0, 0))

    @pl.when(b + 1 < nblk)
    def _():
        _segment_copies(_start, cnt_s, b + 1, copies(b + 1, 1 - slot))

    _segment_copies(_wait, cnt_s, b, copies(b, slot))

    o_ref[...] = x_ref[...]

    def chunk(k, carry):
        r0 = pl.multiple_of(k * LANES, LANES)
        rho = r0 + lax.broadcasted_iota(jnp.int32, (LANES, ROUTE_BLOCK), 0)
        e_lo, e_hi = _chunk_experts(cnt_s, base_s, b, r0)

        def weigh(e, wt):
            return jnp.where(lpos_ref[0, pl.ds(e, 1), :] == rho, aff_ref[0, pl.ds(e, 1), :], wt)

        wt = lax.fori_loop(e_lo, e_hi, weigh, jnp.zeros((LANES, ROUTE_BLOCK), F32))
        w = wt.T
        hi = w.astype(BF16)
        lo = (w - hi.astype(F32)).astype(BF16)
        for f in range(ROW_TILE[0]):
            y = yc_ref[slot, _feature_rows(r0, LANES, f), :].astype(BF16)
            o_ref[:, f * LANES:(f + 1) * LANES] += (
                jnp.dot(hi, y, preferred_element_type=F32)
                + jnp.dot(lo, y, preferred_element_type=F32))
        return carry

    lax.fori_loop(0, (_block_rows(cnt_s, base_s, b) + LANES - 1) // LANES, chunk, 0)
    if final:
        o_ref[...] = _rms(o_ref[...], fg_ref[...])


def combine(x1, lpos_t, aff_t, final_gain, ye, cnt, off, base, cap, final):
    n = x1.shape[0]
    nb = n // ROUTE_BLOCK
    by_block = pl.BlockSpec((1, N_EXPERTS, ROUTE_BLOCK), lambda b, *_: (b, 0, 0))
    grid_spec = pltpu.PrefetchScalarGridSpec(
        num_scalar_prefetch=3,
        grid=(nb,),
        in_specs=[pl.BlockSpec((ROUTE_BLOCK, D_MODEL), lambda b, *_: (b, 0)),
                  by_block, by_block,
                  pl.BlockSpec((1, D_MODEL), lambda b, *_: (0, 0)),
                  pl.BlockSpec(memory_space=pl.ANY)],
        out_specs=pl.BlockSpec((ROUTE_BLOCK, D_MODEL), lambda b, *_: (b, 0)),
        scratch_shapes=[pltpu.VMEM((2, MAX_BLOCK_ROWS * ROW_TILE[0], LANES), F32),
                        pltpu.SemaphoreType.DMA((2,))],
    )
    return pl.pallas_call(
        functools.partial(_combine_kernel, cap=cap, final=final),
        grid_spec=grid_spec,
        out_shape=jax.ShapeDtypeStruct((n, D_MODEL), F32),
        compiler_params=_cparams(("arbitrary",)),
        name="combine",
    )(cnt, off, base, x1, lpos_t, aff_t, final_gain.reshape(1, D_MODEL), ye)


def moe(x1, hn, aff, wg, wu, wd, final_gain, final):
    n = x1.shape[0]
    cap = EC_CAPACITY * n // N_EXPERTS
    lpos_t, aff_t, cnt, off, base = route(aff, cap)
    xe = dispatch(hn, lpos_t, cnt, off, base, cap)
    ye = expert_ffn(xe, wg, wu, wd, cap)
    return combine(x1, lpos_t, aff_t, final_gain, ye, cnt, off, base, cap, final)


def _trunk(x, p):
    batch, seq, _ = x.shape
    n = batch * seq
    x = x.reshape(n, D_MODEL)
    for l in range(DEPTH):
        a_arr, c_arr, *groups = in_proj(x, p["norm_mix"][l], p["w_in"][l], batch, seq)
        qx, k2, vx = mixer_a_prep(a_arr, p["q_gain"][l], p["k_gain"][l], batch, seq)
        oa = mixer_a(qx, k2, vx, batch, seq)
        obs = [mixer_b_group(groups[g], p["rel_bias"], g) for g in range(B_GROUPS)]
        oc = mixer_c(c_arr, p["na_rpb"][l], batch, seq)
        x1, hn, aff = out_proj(oa, obs, oc, x, p["out_gain"][l], p["w_out"][l],
                               p["norm_ffn"][l], p["w_router"][l], seq)
        x = moe(x1, hn, aff, p["w_gate"][l], p["w_up"][l], p["w_down"][l],
                p["final_norm"], final=(l == DEPTH - 1))
    return x.reshape(batch, seq, D_MODEL)


def kernel(x_prompt, x_sample, w_in, w_out, norm_mix, norm_ffn, q_gain, k_gain, out_gain,
           na_rpb, rel_bias, w_router, w_gate, w_up, w_down, final_norm):
    p = dict(w_in=permute_w_in(w_in).astype(BF16), w_out=w_out.astype(BF16), norm_mix=norm_mix,
             norm_ffn=norm_ffn, q_gain=q_gain, k_gain=k_gain, out_gain=out_gain,
             na_rpb=na_rpb, rel_bias=rel_bias, w_router=w_router,
             w_gate=w_gate.astype(BF16), w_up=w_up.astype(BF16), w_down=w_down.astype(BF16),
             final_norm=final_norm)
    return (_trunk(x_prompt, p), _trunk(x_sample, p))
```

```python
import functools
import math

import jax
import jax.numpy as jnp
import numpy as np
from jax import lax
from jax.experimental import pallas as pl
from jax.experimental.pallas import tpu as pltpu

D_MODEL = 1024
DEPTH = 2
GRID_W = 64
HEAD_DIM = 64
EPS = 1e-6
A_HEADS = 8
A_KV_HEADS = 2
A_GROUP = A_HEADS // A_KV_HEADS
ROPE_THETA = 10000.0
DILATED_PAIRS = ((128, 1), (512, 4), (2048, 16))
B_GROUPS = 3
B_HEADS = 4
C_HEADS = 4
NA_ROWS = 8
NA_COLS = 16
N_BUCKETS = 32
MAX_DISTANCE = 2048
N_EXPERTS = 16
EC_CAPACITY = 2
D_EXPERT = 2 * D_MODEL
A_Q = A_HEADS * HEAD_DIM
A_KV = A_KV_HEADS * HEAD_DIM
B_OUT = B_HEADS * HEAD_DIM
C_W = C_HEADS * HEAD_DIM
D_IN = 3840
D_MIX = A_Q + B_OUT + C_W

LANES = 128
PROJ_CHUNK = 768
NEG = -1e30
LOG2E = 1.4426950408889634
VMEM_LIMIT = 56 * 1024 * 1024

F32 = jnp.float32
BF16 = jnp.bfloat16


def _cparams(sem):
    return pltpu.CompilerParams(dimension_semantics=sem, vmem_limit_bytes=VMEM_LIMIT)


def permute_w_in(w_in):
    b0 = A_Q + 2 * A_KV
    bw = B_GROUPS * B_OUT
    parts = [w_in[..., :b0], w_in[..., b0 + 3 * bw:]]
    for g in range(B_GROUPS):
        parts += [w_in[..., b0 + s * bw + g * B_OUT:b0 + s * bw + (g + 1) * B_OUT] for s in range(3)]
    return jnp.concatenate(parts, axis=-1)


def _inproj_kernel(x_ref, g_ref, w_ref, a_ref, c_ref, *rest):
    g_refs, scr_ref = rest[:B_GROUPS], rest[B_GROUPS]
    x = x_ref[...]
    tm = x.shape[0]
    ms = jnp.mean(x * x, axis=-1, keepdims=True)
    h = (x * lax.rsqrt(ms + EPS) * g_ref[...]).astype(BF16)

    def proj(k):
        return jnp.dot(h, w_ref[:, k * PROJ_CHUNK:(k + 1) * PROJ_CHUNK], preferred_element_type=F32)

    a_ref[...] = proj(0).astype(BF16)
    c_ref[...] = proj(1).astype(BF16)
    for g, g_ref_out in enumerate(g_refs):
        d = DILATED_PAIRS[g][1]
        if d == 1:
            g_ref_out[0, 0] = proj(2 + g).astype(BF16)
        else:
            res = proj(2 + g)
            ntile = PROJ_CHUNK // LANES
            for c in range(ntile):
                scr_ref[c] = res[:, c * LANES:(c + 1) * LANES]
            for r in range(d):
                for c in range(ntile):
                    g_ref_out[0, r, :, c * LANES:(c + 1) * LANES] = (
                        scr_ref[c, pl.ds(r, tm // d, stride=d), :].astype(BF16))


def in_proj(x, gain, w_perm_bf16, batch, seq, tm=512):
    n = batch * seq
    nt = seq // tm
    row = pl.BlockSpec((tm, PROJ_CHUNK), lambda i: (i, 0))
    g_specs, g_shapes = [], []
    for _, d in DILATED_PAIRS:
        g_specs.append(pl.BlockSpec((1, d, tm // d, PROJ_CHUNK), lambda i: (i // nt, 0, i % nt, 0)))
        g_shapes.append(jax.ShapeDtypeStruct((batch, d, seq // d, PROJ_CHUNK), BF16))
    return pl.pallas_call(
        _inproj_kernel,
        grid=(n // tm,),
        in_specs=[pl.BlockSpec((tm, D_MODEL), lambda i: (i, 0)),
                  pl.BlockSpec((1, D_MODEL), lambda i: (0, 0)),
                  pl.BlockSpec((D_MODEL, D_IN), lambda i: (0, 0))],
        out_specs=[row, row] + g_specs,
        out_shape=[jax.ShapeDtypeStruct((n, PROJ_CHUNK), BF16)] * 2 + g_shapes,
        scratch_shapes=[pltpu.VMEM((PROJ_CHUNK // LANES, tm, LANES), F32)],
        compiler_params=_cparams(("arbitrary",)),
        name="in_proj",
    )(x, gain.reshape(1, D_MODEL), w_perm_bf16)


def _rope_tables(seq):
    t = np.arange(seq)
    row = (t // GRID_W).astype(np.float64)
    col = (t % GRID_W).astype(np.float64)
    half = HEAD_DIM // 2
    freqs = ROPE_THETA ** (-np.arange(0, half, 2, dtype=np.float64) / half)
    ang = np.concatenate([row[:, None] * freqs, col[:, None] * freqs], axis=-1)
    cos = np.repeat(np.cos(ang), 2, axis=-1)
    sin = np.repeat(np.sin(ang), 2, axis=-1)
    sign = np.tile(np.array([-1.0, 1.0], np.float32), HEAD_DIM // 2)
    cos2 = np.tile(cos, (1, 2)).astype(np.float32)
    sin2 = np.tile(sin * sign, (1, 2)).astype(np.float32)
    return jnp.asarray(cos2), jnp.asarray(sin2)


def _headnorm_rope(x, gain2, cos, sin, seg):
    sq = x * x
    hi = sq.astype(BF16)
    lo = (sq - hi.astype(F32)).astype(BF16)
    ms = (jnp.dot(hi, seg, preferred_element_type=F32)
          + jnp.dot(lo, seg, preferred_element_type=F32))
    y = x * lax.rsqrt(ms + EPS) * gain2
    lane = lax.broadcasted_iota(jnp.int32, y.shape, 1)
    even = (lane % 2) == 0
    swapped = jnp.where(even, pltpu.roll(y, LANES - 1, 1), pltpu.roll(y, 1, 1))
    return y * cos + swapped * sin


def _aprep_kernel(a_ref, cos_ref, sin_ref, qg_ref, kg_ref, q_ref, k_ref, v_ref):
    cos = cos_ref[...]
    sin = sin_ref[...]
    r = lax.broadcasted_iota(jnp.int32, (LANES, LANES), 0) // HEAD_DIM
    c = lax.broadcasted_iota(jnp.int32, (LANES, LANES), 1) // HEAD_DIM
    seg = jnp.where(r == c, 1.0 / HEAD_DIM, 0.0).astype(BF16)
    tm = a_ref.shape[0]
    lane = lax.broadcasted_iota(jnp.int32, (tm, LANES), 1)
    low = lane < HEAD_DIM
    qscale = (HEAD_DIM ** -0.5) * LOG2E
    for t in range(A_Q // LANES):
        x = a_ref[:, t * LANES:(t + 1) * LANES].astype(F32)
        y = _headnorm_rope(x, qg_ref[...], cos, sin, seg) * qscale
        swapped = pltpu.roll(y, HEAD_DIM, 1)
        if (2 * t) // A_GROUP == 0:
            q_even, q_odd = jnp.where(low, y, 0.0), jnp.where(low, swapped, 0.0)
        else:
            q_even, q_odd = jnp.where(low, 0.0, swapped), jnp.where(low, 0.0, y)
        q_ref[0, 2 * t] = q_even.astype(BF16)
        q_ref[0, 2 * t + 1] = q_odd.astype(BF16)
    xk = a_ref[:, A_Q:A_Q + LANES].astype(F32)
    k_ref[0] = _headnorm_rope(xk, kg_ref[...], cos, sin, seg).astype(BF16)
    xv = a_ref[:, A_Q + LANES:A_Q + 2 * LANES].astype(F32)
    one_col = (lane == HEAD_DIM).astype(F32)
    v_ref[0, 0] = (jnp.where(low, xv, 0.0) + one_col).astype(BF16)
    v_ref[0, 1] = (jnp.where(low, pltpu.roll(xv, HEAD_DIM, 1), 0.0) + one_col).astype(BF16)


def mixer_a_prep(a_arr, q_gain, k_gain, batch, seq, tm=512):
    cos2, sin2 = _rope_tables(seq)
    nb = seq // tm
    qg = jnp.tile(q_gain.astype(F32), 2).reshape(1, LANES)
    kg = jnp.tile(k_gain.astype(F32), 2).reshape(1, LANES)
    return pl.pallas_call(
        _aprep_kernel,
        grid=(batch, nb),
        in_specs=[pl.BlockSpec((tm, PROJ_CHUNK), lambda b, i: (b * nb + i, 0)),
                  pl.BlockSpec((tm, LANES), lambda b, i: (i, 0)),
                  pl.BlockSpec((tm, LANES), lambda b, i: (i, 0)),
                  pl.BlockSpec((1, LANES), lambda b, i: (0, 0)),
                  pl.BlockSpec((1, LANES), lambda b, i: (0, 0))],
        out_specs=[pl.BlockSpec((1, A_HEADS, tm, LANES), lambda b, i: (b, 0, i, 0)),
                   pl.BlockSpec((1, tm, LANES), lambda b, i: (b, i, 0)),
                   pl.BlockSpec((1, A_KV_HEADS, tm, LANES), lambda b, i: (b, 0, i, 0))],
        out_shape=[jax.ShapeDtypeStruct((batch, A_HEADS, seq, LANES), BF16),
                   jax.ShapeDtypeStruct((batch, seq, LANES), BF16),
                   jax.ShapeDtypeStruct((batch, A_KV_HEADS, seq, LANES), BF16)],
        compiler_params=_cparams(("arbitrary", "arbitrary")),
        name="mixer_a_prep",
    )(a_arr, cos2, sin2, qg, kg)


FLASH_ROWS = 512


def _flash_kernel(q_ref, k_ref, v_ref, o_ref, m_ref, acc_ref):
    j = pl.program_id(3)
    tq = q_ref.shape[2]
    tk = k_ref.shape[1]

    @pl.when(j == 0)
    def _():
        m_ref[...] = jnp.full(m_ref.shape, -jnp.inf, F32)
        acc_ref[...] = jnp.zeros(acc_ref.shape, F32)

    k = k_ref[0]
    v = v_ref[0, 0]
    for h in range(A_GROUP):
        for r0 in range(0, tq, FLASH_ROWS):
            rows = pl.ds(r0, FLASH_ROWS)
            s = lax.dot_general(q_ref[0, h, rows, :], k, (((1,), (1,)), ((), ())),
                                preferred_element_type=F32)
            m_prev = m_ref[h, rows, :]
            m_new = jnp.maximum(m_prev, jnp.max(s, axis=-1, keepdims=True))
            alpha = jnp.exp2(m_prev - m_new)
            p = jnp.exp2(s - jnp.tile(m_new, (1, tk // LANES))).astype(BF16)
            acc_ref[h, rows, :] = alpha * acc_ref[h, rows, :] + jnp.dot(
                p, v, preferred_element_type=F32)
            m_ref[h, rows, :] = m_new

    @pl.when(j == pl.num_programs(3) - 1)
    def _():
        lane = lax.broadcasted_iota(jnp.int32, (tq, LANES), 1)
        low = lane < HEAD_DIM

        def normed(h):
            acc = acc_ref[h]
            return acc / acc[:, HEAD_DIM:HEAD_DIM + 1]

        for t in range(A_GROUP // 2):
            o_ref[:, t * LANES:(t + 1) * LANES] = jnp.where(
                low, normed(2 * t), pltpu.roll(normed(2 * t + 1), HEAD_DIM, 1)).astype(o_ref.dtype)


def mixer_a(qx, k2, vx, batch, seq, tq=1024, tk=2048):
    nq = seq // tq
    grp_w = A_GROUP * HEAD_DIM
    return pl.pallas_call(
        _flash_kernel,
        grid=(batch, A_KV_HEADS, nq, seq // tk),
        in_specs=[pl.BlockSpec((1, A_GROUP, tq, LANES), lambda b, h, i, j: (b, h, i, 0)),
                  pl.BlockSpec((1, tk, LANES), lambda b, h, i, j: (b, j, 0)),
                  pl.BlockSpec((1, 1, tk, LANES), lambda b, h, i, j: (b, h, j, 0))],
        out_specs=pl.BlockSpec((tq, grp_w), lambda b, h, i, j: (b * nq + i, h)),
        out_shape=jax.ShapeDtypeStruct((batch * seq, A_Q), F32),
        scratch_shapes=[pltpu.VMEM((A_GROUP, tq, LANES), F32),
                        pltpu.VMEM((A_GROUP, tq, LANES), F32)],
        compiler_params=_cparams(("arbitrary",) * 4),
        name="mixer_a_flash",
    )(qx, k2, vx)


B_TL = 128
B_CHAINS = 8


def _toeplitz(vec, nrow, ncol, shift):
    period = vec.shape[-1] + 1
    lead = vec.shape[:-1]
    padded = jnp.concatenate([vec, jnp.zeros(lead + (1,), vec.dtype)], axis=-1)
    flat = jnp.tile(padded, (1,) * len(lead) + (nrow,))[..., :nrow * (period - 1)]
    skew = flat.reshape(lead + (nrow, period - 1))
    return skew[..., shift:shift + ncol]
def _t5_bucket(rel):
    nb = N_BUCKETS // 2
    max_exact = nb // 2
    ret = (rel > 0).astype(jnp.int32) * nb
    n = jnp.abs(rel)
    large = max_exact + (jnp.log(jnp.maximum(n, 1).astype(jnp.float32) / max_exact)
                         / math.log(MAX_DISTANCE / max_exact) * (nb - max_exact)).astype(jnp.int32)
    large = jnp.minimum(large, nb - 1)
    return ret + jnp.where(n < max_exact, n, large)


def _dilated_bias(rel_bias_g, dilation, tl, radius):
    off = np.arange(2 * tl + 2 * radius - 1) - (tl - 1) - radius
    per_off = rel_bias_g[_t5_bucket(jnp.asarray(off * dilation))].astype(F32).T
    per_off = jnp.where(jnp.asarray(np.abs(off) <= radius)[None], per_off, NEG)
    return _toeplitz(per_off, tl, tl + 2 * radius, tl - 1)


def _head_mask(shape, h):
    lane = lax.broadcasted_iota(jnp.int32, shape, 1)
    return (lane < HEAD_DIM) if h % 2 == 0 else (lane >= HEAD_DIM)


def _windowed_heads(q, kwin, vwin, bias_ref, valid, want_lse):
    tl = q.shape[0]
    scale = jnp.asarray(HEAD_DIM ** -0.5, q.dtype)
    sel0 = _head_mask((tl, LANES), 0)
    o_tiles, lse_tiles = [], []
    for t in range(2):
        qt = q[:, t * LANES:(t + 1) * LANES] * scale
        kt = kwin[:, t * LANES:(t + 1) * LANES]
        vt = vwin[:, t * LANES:(t + 1) * LANES]
        zero = jnp.zeros_like(qt)
        q2 = jnp.concatenate([jnp.where(sel0, qt, zero), jnp.where(sel0, zero, qt)], axis=0)
        s = lax.dot_general(q2, kt, (((1,), (1,)), ((), ())), preferred_element_type=F32)
        s = s + jnp.concatenate([bias_ref[2 * t], bias_ref[2 * t + 1]], axis=0)
        if valid is not None:
            s = jnp.where(valid, s, NEG)
        m = jnp.max(s, axis=-1, keepdims=True)
        p = jnp.exp(s - m)
        den = jnp.sum(p, axis=-1, keepdims=True)
        o2 = jnp.dot(p.astype(BF16), vt, preferred_element_type=F32) / den
        o_tiles.append(jnp.where(sel0, o2[:tl], o2[tl:]))
        if want_lse:
            lse2 = m + jnp.log(den)
            lse_tiles.append(jnp.where(sel0, lse2[:tl], lse2[tl:]))
    return o_tiles, lse_tiles


def _dilated_kernel(q_ref, kp_ref, km_ref, kn_ref, vp_ref, vm_ref, vn_ref, bias_ref,
                    o_ref, lse_ref, *, sub_len, radius):
    i = pl.program_id(2)
    nres, tls = q_ref.shape[0], q_ref.shape[1]
    nsub = tls // B_TL
    win = B_TL + 2 * radius
    for r in range(nres):
        kwin = jnp.concatenate([kp_ref[r], km_ref[r], kn_ref[r]], axis=0)
        vwin = jnp.concatenate([vp_ref[r], vm_ref[r], vn_ref[r]], axis=0)
        for j in range(nsub):
            valid = None
            if j == 0 or j == nsub - 1:
                kpos = (i * tls + j * B_TL - radius
                        + lax.broadcasted_iota(jnp.int32, (1, win), 1))
                valid = (kpos >= 0) & (kpos < sub_len)
            rows = slice(j * B_TL, (j + 1) * B_TL)
            o_tiles, lse_tiles = _windowed_heads(
                q_ref[r, rows, :], kwin[j * B_TL:j * B_TL + win], vwin[j * B_TL:j * B_TL + win],
                bias_ref, valid, True)
            for t in range(2):
                o_ref[r, rows, t * LANES:(t + 1) * LANES] = o_tiles[t]
                lse_ref[r, rows, t * LANES:(t + 1) * LANES] = lse_tiles[t]


def mixer_b_group(gqkv, rel_bias, g):
    batch, d, sub_len, _ = gqkv.shape
    window = DILATED_PAIRS[g][0]
    radius = window // (2 * d)
    assert sub_len % B_TL == 0 and B_TL % radius == 0
    nsub = min(B_CHAINS, sub_len // B_TL)
    nres = min(d, B_CHAINS // nsub)
    tls = nsub * B_TL
    hpb = tls // radius
    nhalo = sub_len // radius
    bias = _dilated_bias(rel_bias[:, g * B_HEADS:(g + 1) * B_HEADS], d, B_TL, radius)

    def main(col):
        return pl.BlockSpec((None, nres, tls, B_OUT), lambda b, r, i: (b, r, i, col))

    def prev(col):
        return pl.BlockSpec((None, nres, radius, B_OUT),
                            lambda b, r, i: (b, r, jnp.maximum(i * hpb - 1, 0), col))

    def nxt(col):
        return pl.BlockSpec((None, nres, radius, B_OUT),
                            lambda b, r, i: (b, r, jnp.minimum((i + 1) * hpb, nhalo - 1), col))

    return pl.pallas_call(
        functools.partial(_dilated_kernel, sub_len=sub_len, radius=radius),
        grid=(batch, d // nres, sub_len // tls),
        in_specs=[main(0), prev(1), main(1), nxt(1), prev(2), main(2), nxt(2),
                  pl.BlockSpec((B_HEADS, B_TL, B_TL + 2 * radius), lambda b, r, i: (0, 0, 0))],
        out_specs=[main(0), main(0)],
        out_shape=[jax.ShapeDtypeStruct((batch, d, sub_len, B_OUT), F32)] * 2,
        compiler_params=_cparams(("arbitrary",) * 3),
        name=f"mixer_b_g{g}",
    )(gqkv, gqkv, gqkv, gqkv, gqkv, gqkv, gqkv, bias)


C_QROWS = 4
C_TQ = C_QROWS * GRID_W


def _na_bias(rpb):
    u = np.arange(C_TQ)[:, None] // GRID_W
    cq = np.arange(C_TQ)[:, None] % GRID_W
    w = np.arange(3 * C_TQ)[None, :] // GRID_W
    ck = np.arange(3 * C_TQ)[None, :] % GRID_W
    cs = np.clip(cq - NA_COLS // 2, 0, GRID_W - NA_COLS)
    col_ok = (ck >= cs) & (ck < cs + NA_COLS)
    row_ok = [(w >= C_QROWS) & (w < C_QROWS + NA_ROWS) & (u >= 0),
              (w >= u) & (w < u + NA_ROWS),
              (w >= 0) & (w < NA_ROWS) & (u >= 0)]
    side = GRID_W - NA_COLS
    padded = jnp.pad(rpb.astype(F32), ((0, 0), (0, 0), (side, side)))
    blocks = _toeplitz(padded, GRID_W, GRID_W, GRID_W - 1)
    drow = np.clip((np.arange(3 * C_QROWS)[None, :] - C_QROWS) - np.arange(C_QROWS)[:, None]
                   + NA_ROWS - 1, 0, 2 * NA_ROWS - 2)
    tiles = jnp.take(blocks, jnp.asarray(drow.reshape(-1)), axis=1)
    tiles = tiles.reshape(C_HEADS, C_QROWS, 3 * C_QROWS, GRID_W, GRID_W)
    vals = tiles.transpose(0, 1, 3, 2, 4).reshape(C_HEADS, C_TQ, 3 * C_TQ)
    return jnp.stack([jnp.where(jnp.asarray(r & col_ok)[None], vals, NEG) for r in row_ok])


C_PAIR = 2


def _na_kernel(q_ref, kp_ref, kc_ref, kn_ref, vp_ref, vc_ref, vn_ref, *rest):
    bias_refs, o_ref = rest[:C_PAIR], rest[C_PAIR]
    kall = jnp.concatenate([kp_ref[...], kc_ref[...], kn_ref[...]], axis=0)
    vall = jnp.concatenate([vp_ref[...], vc_ref[...], vn_ref[...]], axis=0)
    for j in range(C_PAIR):
        rows = slice(j * C_TQ, (j + 1) * C_TQ)
        win = slice(j * C_TQ, (j + 3) * C_TQ)
        o_tiles, _ = _windowed_heads(q_ref[rows, :], kall[win], vall[win], bias_refs[j].at[0],
                                     None, False)
        for t in range(2):
            o_ref[rows, t * LANES:(t + 1) * LANES] = o_tiles[t]


def mixer_c(c_arr, rpb, batch, seq):
    rows = seq // GRID_W
    assert rows % (C_PAIR * C_QROWS) == 0 and rows >= 3 * C_QROWS
    nblk = rows // C_QROWS
    nstep = nblk // C_PAIR
    n = batch * seq
    bias = _na_bias(rpb)

    def halo(col, blk_of_step):
        def imap(b, i):
            return (b * nblk + jnp.clip(blk_of_step(i), 0, nblk - 1), col)
        return pl.BlockSpec((C_TQ, C_W), imap)

    def main(col):
        return pl.BlockSpec((C_PAIR * C_TQ, C_W), lambda b, i: (b * nstep + i, col))

    def bias_spec(j):
        def imap(b, i):
            blk = i * C_PAIR + j
            return (jnp.where(blk == 0, 0, jnp.where(blk == nblk - 1, 2, 1)), 0, 0, 0)
        return pl.BlockSpec((1, C_HEADS, C_TQ, 3 * C_TQ), imap)

    before = lambda i: i * C_PAIR - 1
    after = lambda i: (i + 1) * C_PAIR
    return pl.pallas_call(
        _na_kernel,
        grid=(batch, nstep),
        in_specs=[main(0), halo(1, before), main(1), halo(1, after),
                  halo(2, before), main(2), halo(2, after)]
                 + [bias_spec(j) for j in range(C_PAIR)],
        out_specs=main(0),
        out_shape=jax.ShapeDtypeStruct((n, C_W), F32),
        compiler_params=_cparams(("arbitrary", "arbitrary")),
        name="mixer_c",
    )(c_arr, c_arr, c_arr, c_arr, c_arr, c_arr, c_arr, *([bias] * C_PAIR))


def _rms(x, gain):
    ms = jnp.mean(x * x, axis=-1, keepdims=True)
    return x * lax.rsqrt(ms + EPS) * gain


def _token_order(src_ref, scr_ref):
    d, sub = src_ref.shape[1], src_ref.shape[2]
    if d == 1:
        return src_ref[0, 0]
    ntile = scr_ref.shape[0]
    for r in range(d):
        for c in range(ntile):
            scr_ref[c, pl.ds(r, sub, stride=d), :] = src_ref[0, r, :, c * LANES:(c + 1) * LANES]
    return jnp.concatenate([scr_ref[c] for c in range(ntile)], axis=1)


def _outproj_kernel(oa_ref, ob0_ref, ob1_ref, ob2_ref, l0_ref, l1_ref, l2_ref, oc_ref, x_ref,
                    og_ref, w_ref, nf_ref, wr_ref, x1_ref, hn_ref, aff_ref, *scr):
    ob0, ob1, ob2 = (_token_order(r, s) for r, s in zip((ob0_ref, ob1_ref, ob2_ref), scr[:3]))
    l0, l1, l2 = (_token_order(r, s) for r, s in zip((l0_ref, l1_ref, l2_ref), scr[3:]))
    m = jnp.maximum(jnp.maximum(l0, l1), l2)
    e0, e1, e2 = jnp.exp(l0 - m), jnp.exp(l1 - m), jnp.exp(l2 - m)
    ob = (e0 * ob0 + e1 * ob1 + e2 * ob2) / (e0 + e1 + e2)
    og = og_ref[...]
    na = _rms(oa_ref[...], og[:, :A_Q]).astype(BF16)
    nb = _rms(ob, og[:, A_Q:A_Q + B_OUT]).astype(BF16)
    nc = _rms(oc_ref[...], og[:, A_Q + B_OUT:]).astype(BF16)
    mix = (jnp.dot(na, w_ref[:A_Q], preferred_element_type=F32)
           + jnp.dot(nb, w_ref[A_Q:A_Q + B_OUT], preferred_element_type=F32)
           + jnp.dot(nc, w_ref[A_Q + B_OUT:], preferred_element_type=F32))
    x1 = x_ref[...] + mix
    x1_ref[...] = x1
    hn = _rms(x1, nf_ref[...])
    hn_ref[...] = hn.astype(BF16)
    wr = wr_ref[...]
    h_hi, w_hi = hn.astype(BF16), wr.astype(BF16)
    h_lo = (hn - h_hi.astype(F32)).astype(BF16)
    w_lo = (wr - w_hi.astype(F32)).astype(BF16)
    logits = (jnp.dot(h_hi, w_hi, preferred_element_type=F32)
              + jnp.dot(h_lo, w_hi, preferred_element_type=F32)
              + jnp.dot(h_hi, w_lo, preferred_element_type=F32))
    p = jnp.exp(logits - jnp.max(logits, axis=-1, keepdims=True))
    aff_ref[...] = p / jnp.sum(p, axis=-1, keepdims=True)


def out_proj(oa, obs, oc, x, out_gain, w_out_bf16, norm_ffn, w_router, seq, tm=512):
    n = x.shape[0]
    nt = seq // tm
    row = lambda w: pl.BlockSpec((tm, w), lambda i: (i, 0))
    full = lambda a: pl.BlockSpec(a.shape, lambda i: (0,) * a.ndim)
    sub = [pl.BlockSpec((1, d, tm // d, B_OUT), lambda i: (i // nt, 0, i % nt, 0))
           for _, d in DILATED_PAIRS]
    og = out_gain.reshape(1, D_MIX)
    nf = norm_ffn.reshape(1, D_MODEL)
    return pl.pallas_call(
        _outproj_kernel,
        grid=(n // tm,),
        in_specs=[row(A_Q)] + sub + sub + [row(C_W), row(D_MODEL),
                  full(og), full(w_out_bf16), full(nf), full(w_router)],
        out_specs=[row(D_MODEL), row(D_MODEL), row(N_EXPERTS)],
        out_shape=[jax.ShapeDtypeStruct((n, D_MODEL), F32),
                   jax.ShapeDtypeStruct((n, D_MODEL), BF16),
                   jax.ShapeDtypeStruct((n, N_EXPERTS), F32)],
        scratch_shapes=[pltpu.VMEM((B_OUT // LANES, tm, LANES), F32)] * (2 * B_GROUPS),
        compiler_params=_cparams(("arbitrary",)),
        name="out_proj",
    )(oa, obs[0][0], obs[1][0], obs[2][0], obs[0][1], obs[1][1], obs[2][1], oc, x,
      og, w_out_bf16, nf, w_router)


ROUTE_BLOCK = 2 * LANES
ROUTE_ROWS = ROUTE_BLOCK // LANES


def _route_kernel(aff_ref, lpos_ref, cnt_ref, off_ref, base_ref, base_acc, *, cap):
    e = pl.program_id(0)

    @pl.when(e == 0)
    def _():
        base_acc[...] = jnp.zeros(base_acc.shape, F32)

    bits = lax.bitcast_convert_type(aff_ref[0], jnp.int32)
    nb = bits.shape[0]

    def search(i, thr):
        cand = thr | jnp.left_shift(jnp.int32(1), 30 - i)
        cnt = jnp.sum((bits >= cand).astype(F32), keepdims=True)
        return jnp.where(cnt >= cap, cand, thr)

    thr = lax.fori_loop(0, 31, search, jnp.zeros((1, 1), jnp.int32))
    gt = bits > thr
    eq = bits == thr
    need = cap - jnp.sum(gt.astype(F32), keepdims=True)

    ri = lax.broadcasted_iota(jnp.int32, (LANES, LANES), 0)
    ci = lax.broadcasted_iota(jnp.int32, (LANES, LANES), 1)
    upper = (ri <= ci).astype(F32).astype(BF16)
    ones = jnp.ones((LANES, LANES), BF16)
    rr = lax.broadcasted_iota(jnp.int32, (nb, nb), 0)
    rc = lax.broadcasted_iota(jnp.int32, (nb, nb), 1)
    strict_lower = (rc < rr).astype(F32).astype(BF16)
    same_block = (rc // ROUTE_ROWS) == (rr // ROUTE_ROWS)
    block_all = same_block.astype(F32).astype(BF16)
    block_lower = (same_block & (rc < rr)).astype(F32).astype(BF16)

    def prefix(mask_f32):
        mb = mask_f32.astype(BF16)
        in_row = jnp.dot(mb, upper, preferred_element_type=F32)
        tot = jnp.dot(mb, ones, preferred_element_type=F32)
        offs = jnp.dot(strict_lower, tot.astype(BF16), preferred_element_type=F32)
        return in_row, tot, offs

    eq_f = eq.astype(F32)
    in_row_e, _, offs_e = prefix(eq_f)
    rank_eq = offs_e + in_row_e - eq_f
    sel = gt | (eq & (rank_eq < need))
    in_row, tot, offs = prefix(sel.astype(F32))
    tot_b = tot.astype(BF16)
    in_block = jnp.dot(block_lower, tot_b, preferred_element_type=F32)
    block_tot = jnp.dot(block_all, tot_b, preferred_element_type=F32)
    base = base_acc[...]
    lpos_ref[0] = jnp.where(sel, base + in_block + in_row - 1.0, -1.0).astype(jnp.int32)
    cnt_ref[0] = block_tot.astype(jnp.int32)
    off_ref[0] = (offs - in_block).astype(jnp.int32)
    base_ref[0] = base.astype(jnp.int32)
    base_acc[...] = base + block_tot


def route(aff, cap):
    n = aff.shape[0]
    nb = n // LANES
    aff_t = aff.T.reshape(N_EXPERTS, nb, LANES)
    spec = pl.BlockSpec((1, nb, LANES), lambda e: (e, 0, 0))
    outs = pl.pallas_call(
        functools.partial(_route_kernel, cap=cap),
        grid=(N_EXPERTS,),
        in_specs=[spec],
        out_specs=[spec] * 4,
        out_shape=[jax.ShapeDtypeStruct((N_EXPERTS, nb, LANES), jnp.int32)] * 4,
        scratch_shapes=[pltpu.VMEM((nb, LANES), F32)],
        compiler_params=_cparams(("arbitrary",)),
        name="route",
    )(aff_t)
    lpos, cnt, off, base = outs
    nblk = n // ROUTE_BLOCK
    scal = lambda a: a[:, ::ROUTE_ROWS, 0].T.reshape(-1)
    by_block = lambda a: a.reshape(N_EXPERTS, nblk, ROUTE_BLOCK).transpose(1, 0, 2)
    return by_block(lpos), by_block(aff_t), scal(cnt), scal(off), scal(base)


ROW_TILE = (D_MODEL // LANES, LANES)


def _tile_rows(start, rows):
    nf = ROW_TILE[0]
    return pl.ds(pl.multiple_of(start * nf, nf), rows * nf)


def _feature_rows(start, rows, f):
    nf = ROW_TILE[0]
    return pl.ds(start * nf + f, rows, stride=nf)
SEG_PIECE = 16
SEG_BITS = tuple(1 << k for k in range(3, -1, -1))
MAX_BLOCK_ROWS = N_EXPERTS * ROUTE_BLOCK


def _segment_copies(fn, cnt_s, blk, make):
    for e in range(N_EXPERTS):
        c = cnt_s[blk * N_EXPERTS + e]
        npiece = c // SEG_PIECE

        def piece(i, carry):
            fn(make(e, i * SEG_PIECE, SEG_PIECE))
            return carry

        lax.fori_loop(0, npiece, piece, 0)
        rem = c - npiece * SEG_PIECE
        for bit in SEG_BITS:
            done = npiece * SEG_PIECE + (rem // (2 * bit)) * (2 * bit)

            @pl.when((rem & bit) != 0)
            def _():
                fn(make(e, done, bit))


def _wait_rows(total, make_rows):
    for k in range(MAX_BLOCK_ROWS.bit_length() - 1, -1, -1):
        rows = 1 << k

        @pl.when((total & rows) != 0)
        def _():
            make_rows(rows).wait()


def _start(cp):
    cp.start()


def _wait(cp):
    cp.wait()


def _block_rows(cnt_s, base_s, blk):
    last = blk * N_EXPERTS + N_EXPERTS - 1
    return base_s[last] + cnt_s[last]


def _chunk_experts(cnt_s, base_s, blk, r0):
    lo = jnp.int32(0)
    hi = jnp.int32(0)
    for e in range(N_EXPERTS):
        start = base_s[blk * N_EXPERTS + e]
        end = start + cnt_s[blk * N_EXPERTS + e]
        lo = lo + (end <= r0).astype(jnp.int32)
        hi = hi + (start < r0 + LANES).astype(jnp.int32)
    return lo, hi


def _dispatch_kernel(cnt_s, off_s, base_s, h_ref, lpos_ref, xe_hbm, xc_ref, sem, *, cap):
    b = pl.program_id(0)
    nblk = pl.num_programs(0)
    slot = b % 2
    h = h_ref[...]

    def chunk(k, carry):
        r0 = pl.multiple_of(k * LANES, LANES)
        rho = r0 + lax.broadcasted_iota(jnp.int32, (LANES, ROUTE_BLOCK), 0)
        e_lo, e_hi = _chunk_experts(cnt_s, base_s, b, r0)

        def mark(e, hit):
            return jnp.where(lpos_ref[0, pl.ds(e, 1), :] == rho, 1.0, hit)

        hit = lax.fori_loop(e_lo, e_hi, mark, jnp.zeros((LANES, ROUTE_BLOCK), F32))
        onehot = hit.astype(BF16)
        for f in range(ROW_TILE[0]):
            xc_ref[slot, _feature_rows(r0, LANES, f), :] = jnp.dot(
                onehot, h[:, f * LANES:(f + 1) * LANES], preferred_element_type=F32)
        return carry

    lax.fori_loop(0, (_block_rows(cnt_s, base_s, b) + LANES - 1) // LANES, chunk, 0)

    def copies(blk, sl):
        def make(e, done, rows):
            src = base_s[blk * N_EXPERTS + e] + done
            dst = e * cap + off_s[blk * N_EXPERTS + e] + done
            return pltpu.make_async_copy(xc_ref.at[sl, _tile_rows(src, rows)],
                                         xe_hbm.at[_tile_rows(dst, rows)], sem.at[sl])
        return make

    _segment_copies(_start, cnt_s, b, copies(b, slot))

    def drain(sl):
        def make_rows(rows):
            return pltpu.make_async_copy(xc_ref.at[sl, _tile_rows(0, rows)],
                                         xe_hbm.at[_tile_rows(0, rows)], sem.at[sl])
        return make_rows

    @pl.when(b > 0)
    def _():
        _wait_rows(_block_rows(cnt_s, base_s, b - 1), drain(1 - slot))

    @pl.when(b == nblk - 1)
    def _():
        _wait_rows(_block_rows(cnt_s, base_s, b), drain(slot))


def dispatch(hn, lpos_t, cnt, off, base, cap):
    n = hn.shape[0]
    nb = n // ROUTE_BLOCK
    grid_spec = pltpu.PrefetchScalarGridSpec(
        num_scalar_prefetch=3,
        grid=(nb,),
        in_specs=[pl.BlockSpec((ROUTE_BLOCK, D_MODEL), lambda b, *_: (b, 0)),
                  pl.BlockSpec((1, N_EXPERTS, ROUTE_BLOCK), lambda b, *_: (b, 0, 0))],
        out_specs=pl.BlockSpec(memory_space=pl.ANY),
        scratch_shapes=[pltpu.VMEM((2, MAX_BLOCK_ROWS * ROW_TILE[0], LANES), F32),
                        pltpu.SemaphoreType.DMA((2,))],
    )
    return pl.pallas_call(
        functools.partial(_dispatch_kernel, cap=cap),
        grid_spec=grid_spec,
        out_shape=jax.ShapeDtypeStruct((N_EXPERTS * cap * ROW_TILE[0], LANES), F32),
        compiler_params=_cparams(("arbitrary",)),
        name="dispatch",
    )(cnt, off, base, hn, lpos_t)


def _ffn_kernel(xe_ref, wg_ref, wu_ref, wd_ref, ye_ref, xs_ref, *, hid_chunk):
    m = xs_ref.shape[0]
    nf = ROW_TILE[0]
    for f in range(nf):
        xs_ref[:, f * LANES:(f + 1) * LANES] = xe_ref[pl.ds(f, m, stride=nf), :].astype(BF16)
    x = xs_ref[...]
    acc = jnp.zeros((m, D_MODEL), F32)
    for c in range(D_EXPERT // hid_chunk):
        cols = slice(c * hid_chunk, (c + 1) * hid_chunk)
        g = jnp.dot(x, wg_ref[0, :, cols], preferred_element_type=F32)
        u = jnp.dot(x, wu_ref[0, :, cols], preferred_element_type=F32)
        hid = (g * jax.nn.sigmoid(g) * u).astype(BF16)
        acc = acc + jnp.dot(hid, wd_ref[0, cols, :], preferred_element_type=F32)
    for f in range(nf):
        ye_ref[pl.ds(f, m, stride=nf), :] = acc[:, f * LANES:(f + 1) * LANES]


def expert_ffn(xe, wg, wu, wd, cap, tm=512, hid_chunk=512):
    nf = ROW_TILE[0]
    tiles = cap // tm
    io_spec = pl.BlockSpec((tm * nf, LANES), lambda e, j: (e * tiles + j, 0))
    return pl.pallas_call(
        functools.partial(_ffn_kernel, hid_chunk=hid_chunk),
        grid=(N_EXPERTS, tiles),
        in_specs=[io_spec,
                  pl.BlockSpec((1, D_MODEL, D_EXPERT), lambda e, j: (e, 0, 0)),
                  pl.BlockSpec((1, D_MODEL, D_EXPERT), lambda e, j: (e, 0, 0)),
                  pl.BlockSpec((1, D_EXPERT, D_MODEL), lambda e, j: (e, 0, 0))],
        out_specs=io_spec,
        out_shape=jax.ShapeDtypeStruct(xe.shape, F32),
        scratch_shapes=[pltpu.VMEM((tm, D_MODEL), BF16)],
        compiler_params=_cparams(("arbitrary", "arbitrary")),
        name="expert_ffn",
    )(xe, wg, wu, wd)


def _combine_kernel(cnt_s, off_s, base_s, x_ref, lpos_ref, aff_ref, fg_ref, ye_hbm, o_ref,
                    yc_ref, sem, *, cap, final):
    b = pl.program_id(0)
    nblk = pl.num_programs(0)
    slot = b % 2

    def copies(blk, sl):
        def make(e, done, rows):
            src = e * cap + off_s[blk * N_EXPERTS + e] + done
            dst = base_s[blk * N_EXPERTS + e] + done
            return pltpu.make_async_copy(ye_hbm.at[_tile_rows(src, rows)],
                                         yc_ref.at[sl, _tile_rows(dst, rows)], sem.at[sl])
        return make

    @pl.when(b == 0)
    def _():
        yc_ref[...] = jnp.zeros(yc_ref.shape, F32)
        _segment_copies(_start, cnt_s, 0, copies(0, 0))

    @pl.when(b + 1 < nblk)
    def _():
        _segment_copies(_start, cnt_s, b + 1, copies(b + 1, 1 - slot))

    def arrived(rows):
        return pltpu.make_async_copy(ye_hbm.at[_tile_rows(0, rows)],
                                     yc_ref.at[slot, _tile_rows(0, rows)], sem.at[slot])

    _wait_rows(_block_rows(cnt_s, base_s, b), arrived)

    o_ref[...] = x_ref[...]

    def chunk(k, carry):
        r0 = pl.multiple_of(k * LANES, LANES)
        rho = r0 + lax.broadcasted_iota(jnp.int32, (LANES, ROUTE_BLOCK), 0)
        e_lo, e_hi = _chunk_experts(cnt_s, base_s, b, r0)

        def weigh(e, wt):
            return jnp.where(lpos_ref[0, pl.ds(e, 1), :] == rho, aff_ref[0, pl.ds(e, 1), :], wt)

        wt = lax.fori_loop(e_lo, e_hi, weigh, jnp.zeros((LANES, ROUTE_BLOCK), F32))
        w = wt.T
        hi = w.astype(BF16)
        lo = (w - hi.astype(F32)).astype(BF16)
        w2 = jnp.concatenate([hi, lo], axis=1)
        for f in range(ROW_TILE[0]):
            y = yc_ref[slot, _feature_rows(r0, LANES, f), :].astype(BF16)
            o_ref[:, f * LANES:(f + 1) * LANES] += jnp.dot(
                w2, jnp.concatenate([y, y], axis=0), preferred_element_type=F32)
        return carry

    lax.fori_loop(0, (_block_rows(cnt_s, base_s, b) + LANES - 1) // LANES, chunk, 0)
    if final:
        o_ref[...] = _rms(o_ref[...], fg_ref[...])


def combine(x1, lpos_t, aff_t, final_gain, ye, cnt, off, base, cap, final):
    n = x1.shape[0]
    nb = n // ROUTE_BLOCK
    by_block = pl.BlockSpec((1, N_EXPERTS, ROUTE_BLOCK), lambda b, *_: (b, 0, 0))
    grid_spec = pltpu.PrefetchScalarGridSpec(
        num_scalar_prefetch=3,
        grid=(nb,),
        in_specs=[pl.BlockSpec((ROUTE_BLOCK, D_MODEL), lambda b, *_: (b, 0)),
                  by_block, by_block,
                  pl.BlockSpec((1, D_MODEL), lambda b, *_: (0, 0)),
                  pl.BlockSpec(memory_space=pl.ANY)],
        out_specs=pl.BlockSpec((ROUTE_BLOCK, D_MODEL), lambda b, *_: (b, 0)),
        scratch_shapes=[pltpu.VMEM((2, MAX_BLOCK_ROWS * ROW_TILE[0], LANES), F32),
                        pltpu.SemaphoreType.DMA((2,))],
    )
    return pl.pallas_call(
        functools.partial(_combine_kernel, cap=cap, final=final),
        grid_spec=grid_spec,
        out_shape=jax.ShapeDtypeStruct((n, D_MODEL), F32),
        compiler_params=_cparams(("arbitrary",)),
        name="combine",
    )(cnt, off, base, x1, lpos_t, aff_t, final_gain.reshape(1, D_MODEL), ye)


def moe(x1, hn, aff, wg, wu, wd, final_gain, final):
    n = x1.shape[0]
    cap = EC_CAPACITY * n // N_EXPERTS
    lpos_t, aff_t, cnt, off, base = route(aff, cap)
    xe = dispatch(hn, lpos_t, cnt, off, base, cap)
    ye = expert_ffn(xe, wg, wu, wd, cap)
    return combine(x1, lpos_t, aff_t, final_gain, ye, cnt, off, base, cap, final)


def _trunk(x, p):
    batch, seq, _ = x.shape
    n = batch * seq
    x = x.reshape(n, D_MODEL)
    for l in range(DEPTH):
        a_arr, c_arr, *groups = in_proj(x, p["norm_mix"][l], p["w_in"][l], batch, seq)
        qx, k2, vx = mixer_a_prep(a_arr, p["q_gain"][l], p["k_gain"][l], batch, seq)
        oa = mixer_a(qx, k2, vx, batch, seq)
        obs = [mixer_b_group(groups[g], p["rel_bias"], g) for g in range(B_GROUPS)]
        oc = mixer_c(c_arr, p["na_rpb"][l], batch, seq)
        x1, hn, aff = out_proj(oa, obs, oc, x, p["out_gain"][l], p["w_out"][l],
                               p["norm_ffn"][l], p["w_router"][l], seq)
        x = moe(x1, hn, aff, p["w_gate"][l], p["w_up"][l], p["w_down"][l],
                p["final_norm"], final=(l == DEPTH - 1))
    return x.reshape(batch, seq, D_MODEL)


def kernel(x_prompt, x_sample, w_in, w_out, norm_mix, norm_ffn, q_gain, k_gain, out_gain,
           na_rpb, rel_bias, w_router, w_gate, w_up, w_down, final_norm):
    p = dict(w_in=permute_w_in(w_in).astype(BF16), w_out=w_out.astype(BF16), norm_mix=norm_mix,
             norm_ffn=norm_ffn, q_gain=q_gain, k_gain=k_gain, out_gain=out_gain,
             na_rpb=na_rpb, rel_bias=rel_bias, w_router=w_router,
             w_gate=w_gate.astype(BF16), w_up=w_up.astype(BF16), w_down=w_down.astype(BF16),
             final_norm=final_norm)
    return (_trunk(x_prompt, p), _trunk(x_sample, p))
```

```python
import functools
import math

import jax
import jax.numpy as jnp
import numpy as np
from jax import lax
from jax.experimental import pallas as pl
from jax.experimental.pallas import tpu as pltpu

D_MODEL = 1024
DEPTH = 2
GRID_W = 64
HEAD_DIM = 64
EPS = 1e-6
A_HEADS = 8
A_KV_HEADS = 2
A_GROUP = A_HEADS // A_KV_HEADS
ROPE_THETA = 10000.0
DILATED_PAIRS = ((128, 1), (512, 4), (2048, 16))
B_GROUPS = 3
B_HEADS = 4
C_HEADS = 4
NA_ROWS = 8
NA_COLS = 16
N_BUCKETS = 32
MAX_DISTANCE = 2048
N_EXPERTS = 16
EC_CAPACITY = 2
D_EXPERT = 2 * D_MODEL
A_Q = A_HEADS * HEAD_DIM
A_KV = A_KV_HEADS * HEAD_DIM
B_OUT = B_HEADS * HEAD_DIM
C_W = C_HEADS * HEAD_DIM
D_IN = 3840
D_MIX = A_Q + B_OUT + C_W

LANES = 128
PROJ_CHUNK = 768
NEG = -1e30
LOG2E = 1.4426950408889634
VMEM_LIMIT = 56 * 1024 * 1024

F32 = jnp.float32
BF16 = jnp.bfloat16


def _cparams(sem):
    return pltpu.CompilerParams(dimension_semantics=sem, vmem_limit_bytes=VMEM_LIMIT)


def permute_w_in(w_in):
    b0 = A_Q + 2 * A_KV
    bw = B_GROUPS * B_OUT
    parts = [w_in[..., :b0], w_in[..., b0 + 3 * bw:]]
    for g in range(B_GROUPS):
        parts += [w_in[..., b0 + s * bw + g * B_OUT:b0 + s * bw + (g + 1) * B_OUT] for s in range(3)]
    return jnp.concatenate(parts, axis=-1)


def _inproj_kernel(x_ref, g_ref, w_ref, a_ref, c_ref, *rest):
    g_refs, scr_ref = rest[:B_GROUPS], rest[B_GROUPS]
    x = x_ref[...]
    tm = x.shape[0]
    ms = jnp.mean(x * x, axis=-1, keepdims=True)
    h = (x * lax.rsqrt(ms + EPS) * g_ref[...]).astype(BF16)

    def proj(k):
        return jnp.dot(h, w_ref[:, k * PROJ_CHUNK:(k + 1) * PROJ_CHUNK], preferred_element_type=F32)

    a_ref[...] = proj(0).astype(BF16)
    c_ref[...] = proj(1).astype(BF16)
    for g, g_ref_out in enumerate(g_refs):
        d = DILATED_PAIRS[g][1]
        if d == 1:
            g_ref_out[0, 0] = proj(2 + g).astype(BF16)
        else:
            res = proj(2 + g)
            ntile = PROJ_CHUNK // LANES
            for c in range(ntile):
                scr_ref[c] = res[:, c * LANES:(c + 1) * LANES]
            for r in range(d):
                for c in range(ntile):
                    g_ref_out[0, r, :, c * LANES:(c + 1) * LANES] = (
                        scr_ref[c, pl.ds(r, tm // d, stride=d), :].astype(BF16))


def in_proj(x, gain, w_perm_bf16, batch, seq, tm=512):
    n = batch * seq
    nt = seq // tm
    row = pl.BlockSpec((tm, PROJ_CHUNK), lambda i: (i, 0))
    g_specs, g_shapes = [], []
    for _, d in DILATED_PAIRS:
        g_specs.append(pl.BlockSpec((1, d, tm // d, PROJ_CHUNK), lambda i: (i // nt, 0, i % nt, 0)))
        g_shapes.append(jax.ShapeDtypeStruct((batch, d, seq // d, PROJ_CHUNK), BF16))
    return pl.pallas_call(
        _inproj_kernel,
        grid=(n // tm,),
        in_specs=[pl.BlockSpec((tm, D_MODEL), lambda i: (i, 0)),
                  pl.BlockSpec((1, D_MODEL), lambda i: (0, 0)),
                  pl.BlockSpec((D_MODEL, D_IN), lambda i: (0, 0))],
        out_specs=[row, row] + g_specs,
        out_shape=[jax.ShapeDtypeStruct((n, PROJ_CHUNK), BF16)] * 2 + g_shapes,
        scratch_shapes=[pltpu.VMEM((PROJ_CHUNK // LANES, tm, LANES), F32)],
        compiler_params=_cparams(("arbitrary",)),
        name="in_proj",
    )(x, gain.reshape(1, D_MODEL), w_perm_bf16)


def _rope_tables(seq):
    t = np.arange(seq)
    row = (t // GRID_W).astype(np.float64)
    col = (t % GRID_W).astype(np.float64)
    half = HEAD_DIM // 2
    freqs = ROPE_THETA ** (-np.arange(0, half, 2, dtype=np.float64) / half)
    ang = np.concatenate([row[:, None] * freqs, col[:, None] * freqs], axis=-1)
    cos = np.repeat(np.cos(ang), 2, axis=-1)
    sin = np.repeat(np.sin(ang), 2, axis=-1)
    sign = np.tile(np.array([-1.0, 1.0], np.float32), HEAD_DIM // 2)
    cos2 = np.tile(cos, (1, 2)).astype(np.float32)
    sin2 = np.tile(sin * sign, (1, 2)).astype(np.float32)
    return jnp.asarray(cos2), jnp.asarray(sin2)


def _headnorm_rope(x, gain2, cos, sin, seg):
    sq = x * x
    hi = sq.astype(BF16)
    lo = (sq - hi.astype(F32)).astype(BF16)
    ms = (jnp.dot(hi, seg, preferred_element_type=F32)
          + jnp.dot(lo, seg, preferred_element_type=F32))
    y = x * lax.rsqrt(ms + EPS) * gain2
    lane = lax.broadcasted_iota(jnp.int32, y.shape, 1)
    even = (lane % 2) == 0
    swapped = jnp.where(even, pltpu.roll(y, LANES - 1, 1), pltpu.roll(y, 1, 1))
    return y * cos + swapped * sin


def _aprep_kernel(a_ref, cos_ref, sin_ref, qg_ref, kg_ref, q_ref, k_ref, v_ref):
    cos = cos_ref[...]
    sin = sin_ref[...]
    r = lax.broadcasted_iota(jnp.int32, (LANES, LANES), 0) // HEAD_DIM
    c = lax.broadcasted_iota(jnp.int32, (LANES, LANES), 1) // HEAD_DIM
    seg = jnp.where(r == c, 1.0 / HEAD_DIM, 0.0).astype(BF16)
    tm = a_ref.shape[0]
    lane = lax.broadcasted_iota(jnp.int32, (tm, LANES), 1)
    low = lane < HEAD_DIM
    qscale = (HEAD_DIM ** -0.5) * LOG2E
    for t in range(A_Q // LANES):
        x = a_ref[:, t * LANES:(t + 1) * LANES].astype(F32)
        y = _headnorm_rope(x, qg_ref[...], cos, sin, seg) * qscale
        swapped = pltpu.roll(y, HEAD_DIM, 1)
        if (2 * t) // A_GROUP == 0:
            q_even, q_odd = jnp.where(low, y, 0.0), jnp.where(low, swapped, 0.0)
        else:
            q_even, q_odd = jnp.where(low, 0.0, swapped), jnp.where(low, 0.0, y)
        q_ref[0, 2 * t] = q_even.astype(BF16)
        q_ref[0, 2 * t + 1] = q_odd.astype(BF16)
    xk = a_ref[:, A_Q:A_Q + LANES].astype(F32)
    k_ref[0] = _headnorm_rope(xk, kg_ref[...], cos, sin, seg).astype(BF16)
    xv = a_ref[:, A_Q + LANES:A_Q + 2 * LANES].astype(F32)
    one_col = (lane == HEAD_DIM).astype(F32)
    v_ref[0, 0] = (jnp.where(low, xv, 0.0) + one_col).astype(BF16)
    v_ref[0, 1] = (jnp.where(low, pltpu.roll(xv, HEAD_DIM, 1), 0.0) + one_col).astype(BF16)


def mixer_a_prep(a_arr, q_gain, k_gain, batch, seq, tm=512):
    cos2, sin2 = _rope_tables(seq)
    nb = seq // tm
    qg = jnp.tile(q_gain.astype(F32), 2).reshape(1, LANES)
    kg = jnp.tile(k_gain.astype(F32), 2).reshape(1, LANES)
    return pl.pallas_call(
        _aprep_kernel,
        grid=(batch, nb),
        in_specs=[pl.BlockSpec((tm, PROJ_CHUNK), lambda b, i: (b * nb + i, 0)),
                  pl.BlockSpec((tm, LANES), lambda b, i: (i, 0)),
                  pl.BlockSpec((tm, LANES), lambda b, i: (i, 0)),
                  pl.BlockSpec((1, LANES), lambda b, i: (0, 0)),
                  pl.BlockSpec((1, LANES), lambda b, i: (0, 0))],
        out_specs=[pl.BlockSpec((1, A_HEADS, tm, LANES), lambda b, i: (b, 0, i, 0)),
                   pl.BlockSpec((1, tm, LANES), lambda b, i: (b, i, 0)),
                   pl.BlockSpec((1, A_KV_HEADS, tm, LANES), lambda b, i: (b, 0, i, 0))],
        out_shape=[jax.ShapeDtypeStruct((batch, A_HEADS, seq, LANES), BF16),
                   jax.ShapeDtypeStruct((batch, seq, LANES), BF16),
                   jax.ShapeDtypeStruct((batch, A_KV_HEADS, seq, LANES), BF16)],
        compiler_params=_cparams(("arbitrary", "arbitrary")),
        name="mixer_a_prep",
    )(a_arr, cos2, sin2, qg, kg)


FLASH_ROWS = 1024


def _flash_kernel(q_ref, k_ref, v_ref, o_ref, m_ref, acc_ref):
    j = pl.program_id(3)
    tq = q_ref.shape[2]
    tk = k_ref.shape[1]

    @pl.when(j == 0)
    def _():
        m_ref[...] = jnp.full(m_ref.shape, -jnp.inf, F32)
        acc_ref[...] = jnp.zeros(acc_ref.shape, F32)

    k = k_ref[0]
    v = v_ref[0, 0]
    for h in range(A_GROUP):
        for r0 in range(0, tq, FLASH_ROWS):
            rows = pl.ds(r0, FLASH_ROWS)
            s = lax.dot_general(q_ref[0, h, rows, :], k, (((1,), (1,)), ((), ())),
                                preferred_element_type=F32)
            m_prev = m_ref[h, rows, :]
            m_new = jnp.maximum(m_prev, jnp.max(s, axis=-1, keepdims=True))
            alpha = jnp.exp2(m_prev - m_new)
            p = jnp.exp2(s - jnp.tile(m_new, (1, tk // LANES))).astype(BF16)
            acc_ref[h, rows, :] = alpha * acc_ref[h, rows, :] + jnp.dot(
                p, v, preferred_element_type=F32)
            m_ref[h, rows, :] = m_new

    @pl.when(j == pl.num_programs(3) - 1)
    def _():
        lane = lax.broadcasted_iota(jnp.int32, (tq, LANES), 1)
        low = lane < HEAD_DIM

        def normed(h):
            acc = acc_ref[h]
            return acc / acc[:, HEAD_DIM:HEAD_DIM + 1]

        for t in range(A_GROUP // 2):
            o_ref[:, t * LANES:(t + 1) * LANES] = jnp.where(
                low, normed(2 * t), pltpu.roll(normed(2 * t + 1), HEAD_DIM, 1)).astype(o_ref.dtype)


def mixer_a(qx, k2, vx, batch, seq, tq=1024, tk=2048):
    nq = seq // tq
    grp_w = A_GROUP * HEAD_DIM
    return pl.pallas_call(
        _flash_kernel,
        grid=(batch, A_KV_HEADS, nq, seq // tk),
        in_specs=[pl.BlockSpec((1, A_GROUP, tq, LANES), lambda b, h, i, j: (b, h, i, 0)),
                  pl.BlockSpec((1, tk, LANES), lambda b, h, i, j: (b, j, 0)),
                  pl.BlockSpec((1, 1, tk, LANES), lambda b, h, i, j: (b, h, j, 0))],
        out_specs=pl.BlockSpec((tq, grp_w), lambda b, h, i, j: (b * nq + i, h)),
        out_shape=jax.ShapeDtypeStruct((batch * seq, A_Q), F32),
        scratch_shapes=[pltpu.VMEM((A_GROUP, tq, LANES), F32),
                        pltpu.VMEM((A_GROUP, tq, LANES), F32)],
        compiler_params=_cparams(("arbitrary",) * 4),
        name="mixer_a_flash",
    )(qx, k2, vx)


B_TL = 128
B_CHAINS = 16


def _toeplitz(vec, nrow, ncol, shift):
    period = vec.shape[-1] + 1
    lead = vec.shape[:-1]
    padded = jnp.concatenate([vec, jnp.zeros(lead + (1,), vec.dtype)], axis=-1)
    flat = jnp.tile(padded, (1,) * len(lead) + (nrow,))[..., :nrow * (period - 1)]
    skew = flat.reshape(lead + (nrow, period - 1))
    return skew[..., shift:shift + ncol]
def _t5_bucket(rel):
    nb = N_BUCKETS // 2
    max_exact = nb // 2
    ret = (rel > 0).astype(jnp.int32) * nb
    n = jnp.abs(rel)
    large = max_exact + (jnp.log(jnp.maximum(n, 1).astype(jnp.float32) / max_exact)
                         / math.log(MAX_DISTANCE / max_exact) * (nb - max_exact)).astype(jnp.int32)
    large = jnp.minimum(large, nb - 1)
    return ret + jnp.where(n < max_exact, n, large)


def _dilated_bias(rel_bias_g, dilation, tl, radius):
    off = np.arange(2 * tl + 2 * radius - 1) - (tl - 1) - radius
    per_off = rel_bias_g[_t5_bucket(jnp.asarray(off * dilation))].astype(F32).T
    per_off = jnp.where(jnp.asarray(np.abs(off) <= radius)[None], per_off, NEG)
    return _toeplitz(per_off, tl, tl + 2 * radius, tl - 1)


def _head_mask(shape, h):
    lane = lax.broadcasted_iota(jnp.int32, shape, 1)
    return (lane < HEAD_DIM) if h % 2 == 0 else (lane >= HEAD_DIM)


def _windowed_heads(q, kwin, vwin, bias_ref, valid, want_lse):
    tl = q.shape[0]
    scale = jnp.asarray(HEAD_DIM ** -0.5, q.dtype)
    sel0 = _head_mask((tl, LANES), 0)
    o_tiles, lse_tiles = [], []
    for t in range(2):
        qt = q[:, t * LANES:(t + 1) * LANES] * scale
        kt = kwin[:, t * LANES:(t + 1) * LANES]
        vt = vwin[:, t * LANES:(t + 1) * LANES]
        zero = jnp.zeros_like(qt)
        q2 = jnp.concatenate([jnp.where(sel0, qt, zero), jnp.where(sel0, zero, qt)], axis=0)
        s = lax.dot_general(q2, kt, (((1,), (1,)), ((), ())), preferred_element_type=F32)
        s = s + jnp.concatenate([bias_ref[2 * t], bias_ref[2 * t + 1]], axis=0)
        if valid is not None:
            s = jnp.where(valid, s, NEG)
        m = jnp.max(s, axis=-1, keepdims=True)
        p = jnp.exp(s - m)
        den = jnp.sum(p, axis=-1, keepdims=True)
        o2 = jnp.dot(p.astype(BF16), vt, preferred_element_type=F32) / den
        o_tiles.append(jnp.where(sel0, o2[:tl], o2[tl:]))
        if want_lse:
            lse2 = m + jnp.log(den)
            lse_tiles.append(jnp.where(sel0, lse2[:tl], lse2[tl:]))
    return o_tiles, lse_tiles


def _dilated_kernel(q_ref, kp_ref, km_ref, kn_ref, vp_ref, vm_ref, vn_ref, bias_ref,
                    o_ref, lse_ref, *, sub_len, radius):
    i = pl.program_id(2)
    nres, tls = q_ref.shape[0], q_ref.shape[1]
    nsub = tls // B_TL
    win = B_TL + 2 * radius
    for r in range(nres):
        kwin = jnp.concatenate([kp_ref[r], km_ref[r], kn_ref[r]], axis=0)
        vwin = jnp.concatenate([vp_ref[r], vm_ref[r], vn_ref[r]], axis=0)
        for j in range(nsub):
            valid = None
            if j == 0 or j == nsub - 1:
                kpos = (i * tls + j * B_TL - radius
                        + lax.broadcasted_iota(jnp.int32, (1, win), 1))
                valid = (kpos >= 0) & (kpos < sub_len)
            rows = slice(j * B_TL, (j + 1) * B_TL)
            o_tiles, lse_tiles = _windowed_heads(
                q_ref[r, rows, :], kwin[j * B_TL:j * B_TL + win], vwin[j * B_TL:j * B_TL + win],
                bias_ref, valid, True)
            for t in range(2):
                o_ref[r, rows, t * LANES:(t + 1) * LANES] = o_tiles[t]
                lse_ref[r, rows, t * LANES:(t + 1) * LANES] = lse_tiles[t]


def mixer_b_group(gqkv, rel_bias, g):
    batch, d, sub_len, _ = gqkv.shape
    window = DILATED_PAIRS[g][0]
    radius = window // (2 * d)
    assert sub_len % B_TL == 0 and B_TL % radius == 0
    nsub = min(B_CHAINS, sub_len // B_TL)
    nres = min(d, B_CHAINS // nsub)
    tls = nsub * B_TL
    hpb = tls // radius
    nhalo = sub_len // radius
    bias = _dilated_bias(rel_bias[:, g * B_HEADS:(g + 1) * B_HEADS], d, B_TL, radius)

    def main(col):
        return pl.BlockSpec((None, nres, tls, B_OUT), lambda b, r, i: (b, r, i, col))

    def prev(col):
        return pl.BlockSpec((None, nres, radius, B_OUT),
                            lambda b, r, i: (b, r, jnp.maximum(i * hpb - 1, 0), col))

    def nxt(col):
        return pl.BlockSpec((None, nres, radius, B_OUT),
                            lambda b, r, i: (b, r, jnp.minimum((i + 1) * hpb, nhalo - 1), col))

    return pl.pallas_call(
        functools.partial(_dilated_kernel, sub_len=sub_len, radius=radius),
        grid=(batch, d // nres, sub_len // tls),
        in_specs=[main(0), prev(1), main(1), nxt(1), prev(2), main(2), nxt(2),
                  pl.BlockSpec((B_HEADS, B_TL, B_TL + 2 * radius), lambda b, r, i: (0, 0, 0))],
        out_specs=[main(0), main(0)],
        out_shape=[jax.ShapeDtypeStruct((batch, d, sub_len, B_OUT), F32)] * 2,
        compiler_params=_cparams(("arbitrary",) * 3),
        name=f"mixer_b_g{g}",
    )(gqkv, gqkv, gqkv, gqkv, gqkv, gqkv, gqkv, bias)


C_QROWS = 4
C_TQ = C_QROWS * GRID_W


def _na_bias(rpb):
    u = np.arange(C_TQ)[:, None] // GRID_W
    cq = np.arange(C_TQ)[:, None] % GRID_W
    w = np.arange(3 * C_TQ)[None, :] // GRID_W
    ck = np.arange(3 * C_TQ)[None, :] % GRID_W
    cs = np.clip(cq - NA_COLS // 2, 0, GRID_W - NA_COLS)
    col_ok = (ck >= cs) & (ck < cs + NA_COLS)
    row_ok = [(w >= C_QROWS) & (w < C_QROWS + NA_ROWS) & (u >= 0),
              (w >= u) & (w < u + NA_ROWS),
              (w >= 0) & (w < NA_ROWS) & (u >= 0)]
    side = GRID_W - NA_COLS
    padded = jnp.pad(rpb.astype(F32), ((0, 0), (0, 0), (side, side)))
    blocks = _toeplitz(padded, GRID_W, GRID_W, GRID_W - 1)
    drow = np.clip((np.arange(3 * C_QROWS)[None, :] - C_QROWS) - np.arange(C_QROWS)[:, None]
                   + NA_ROWS - 1, 0, 2 * NA_ROWS - 2)
    tiles = jnp.take(blocks, jnp.asarray(drow.reshape(-1)), axis=1)
    tiles = tiles.reshape(C_HEADS, C_QROWS, 3 * C_QROWS, GRID_W, GRID_W)
    vals = tiles.transpose(0, 1, 3, 2, 4).reshape(C_HEADS, C_TQ, 3 * C_TQ)
    return jnp.stack([jnp.where(jnp.asarray(r & col_ok)[None], vals, NEG) for r in row_ok])


C_PAIR = 4


def _na_kernel(q_ref, kp_ref, kc_ref, kn_ref, vp_ref, vc_ref, vn_ref, *rest):
    bias_refs, o_ref = rest[:C_PAIR], rest[C_PAIR]
    kall = jnp.concatenate([kp_ref[...], kc_ref[...], kn_ref[...]], axis=0)
    vall = jnp.concatenate([vp_ref[...], vc_ref[...], vn_ref[...]], axis=0)
    for j in range(C_PAIR):
        rows = slice(j * C_TQ, (j + 1) * C_TQ)
        win = slice(j * C_TQ, (j + 3) * C_TQ)
        o_tiles, _ = _windowed_heads(q_ref[rows, :], kall[win], vall[win], bias_refs[j].at[0],
                                     None, False)
        for t in range(2):
            o_ref[rows, t * LANES:(t + 1) * LANES] = o_tiles[t]


def mixer_c(c_arr, rpb, batch, seq):
    rows = seq // GRID_W
    assert rows % (C_PAIR * C_QROWS) == 0 and rows >= 3 * C_QROWS
    nblk = rows // C_QROWS
    nstep = nblk // C_PAIR
    n = batch * seq
    bias = _na_bias(rpb)

    def halo(col, blk_of_step):
        def imap(b, i):
            return (b * nblk + jnp.clip(blk_of_step(i), 0, nblk - 1), col)
        return pl.BlockSpec((C_TQ, C_W), imap)

    def main(col):
        return pl.BlockSpec((C_PAIR * C_TQ, C_W), lambda b, i: (b * nstep + i, col))

    def bias_spec(j):
        def imap(b, i):
            blk = i * C_PAIR + j
            return (jnp.where(blk == 0, 0, jnp.where(blk == nblk - 1, 2, 1)), 0, 0, 0)
        return pl.BlockSpec((1, C_HEADS, C_TQ, 3 * C_TQ), imap)

    before = lambda i: i * C_PAIR - 1
    after = lambda i: (i + 1) * C_PAIR
    return pl.pallas_call(
        _na_kernel,
        grid=(batch, nstep),
        in_specs=[main(0), halo(1, before), main(1), halo(1, after),
                  halo(2, before), main(2), halo(2, after)]
                 + [bias_spec(j) for j in range(C_PAIR)],
        out_specs=main(0),
        out_shape=jax.ShapeDtypeStruct((n, C_W), F32),
        compiler_params=_cparams(("arbitrary", "arbitrary")),
        name="mixer_c",
    )(c_arr, c_arr, c_arr, c_arr, c_arr, c_arr, c_arr, *([bias] * C_PAIR))


def _rms(x, gain):
    ms = jnp.mean(x * x, axis=-1, keepdims=True)
    return x * lax.rsqrt(ms + EPS) * gain


def _token_order(src_ref, scr_ref):
    d, sub = src_ref.shape[1], src_ref.shape[2]
    if d == 1:
        return src_ref[0, 0]
    ntile = scr_ref.shape[0]
    for r in range(d):
        for c in range(ntile):
            scr_ref[c, pl.ds(r, sub, stride=d), :] = src_ref[0, r, :, c * LANES:(c + 1) * LANES]
    return jnp.concatenate([scr_ref[c] for c in range(ntile)], axis=1)


def _outproj_kernel(oa_ref, ob0_ref, ob1_ref, ob2_ref, l0_ref, l1_ref, l2_ref, oc_ref, x_ref,
                    og_ref, w_ref, nf_ref, wr_ref, x1_ref, hn_ref, aff_ref, *scr):
    ob0, ob1, ob2 = (_token_order(r, s) for r, s in zip((ob0_ref, ob1_ref, ob2_ref), scr[:3]))
    l0, l1, l2 = (_token_order(r, s) for r, s in zip((l0_ref, l1_ref, l2_ref), scr[3:]))
    m = jnp.maximum(jnp.maximum(l0, l1), l2)
    e0, e1, e2 = jnp.exp(l0 - m), jnp.exp(l1 - m), jnp.exp(l2 - m)
    ob = (e0 * ob0 + e1 * ob1 + e2 * ob2) / (e0 + e1 + e2)
    og = og_ref[...]
    na = _rms(oa_ref[...], og[:, :A_Q]).astype(BF16)
    nb = _rms(ob, og[:, A_Q:A_Q + B_OUT]).astype(BF16)
    nc = _rms(oc_ref[...], og[:, A_Q + B_OUT:]).astype(BF16)
    mix = (jnp.dot(na, w_ref[:A_Q], preferred_element_type=F32)
           + jnp.dot(nb, w_ref[A_Q:A_Q + B_OUT], preferred_element_type=F32)
           + jnp.dot(nc, w_ref[A_Q + B_OUT:], preferred_element_type=F32))
    x1 = x_ref[...] + mix
    x1_ref[...] = x1
    hn = _rms(x1, nf_ref[...])
    hn_ref[...] = hn.astype(BF16)
    wr = wr_ref[...]
    h_hi, w_hi = hn.astype(BF16), wr.astype(BF16)
    h_lo = (hn - h_hi.astype(F32)).astype(BF16)
    w_lo = (wr - w_hi.astype(F32)).astype(BF16)
    logits = (jnp.dot(h_hi, w_hi, preferred_element_type=F32)
              + jnp.dot(h_lo, w_hi, preferred_element_type=F32)
              + jnp.dot(h_hi, w_lo, preferred_element_type=F32))
    p = jnp.exp(logits - jnp.max(logits, axis=-1, keepdims=True))
    aff_ref[...] = p / jnp.sum(p, axis=-1, keepdims=True)


def out_proj(oa, obs, oc, x, out_gain, w_out_bf16, norm_ffn, w_router, seq, tm=512):
    n = x.shape[0]
    nt = seq // tm
    row = lambda w: pl.BlockSpec((tm, w), lambda i: (i, 0))
    full = lambda a: pl.BlockSpec(a.shape, lambda i: (0,) * a.ndim)
    sub = [pl.BlockSpec((1, d, tm // d, B_OUT), lambda i: (i // nt, 0, i % nt, 0))
           for _, d in DILATED_PAIRS]
    og = out_gain.reshape(1, D_MIX)
    nf = norm_ffn.reshape(1, D_MODEL)
    return pl.pallas_call(
        _outproj_kernel,
        grid=(n // tm,),
        in_specs=[row(A_Q)] + sub + sub + [row(C_W), row(D_MODEL),
                  full(og), full(w_out_bf16), full(nf), full(w_router)],
        out_specs=[row(D_MODEL), row(D_MODEL), row(N_EXPERTS)],
        out_shape=[jax.ShapeDtypeStruct((n, D_MODEL), F32),
                   jax.ShapeDtypeStruct((n, D_MODEL), BF16),
                   jax.ShapeDtypeStruct((n, N_EXPERTS), F32)],
        scratch_shapes=[pltpu.VMEM((B_OUT // LANES, tm, LANES), F32)] * (2 * B_GROUPS),
        compiler_params=_cparams(("arbitrary",)),
        name="out_proj",
    )(oa, obs[0][0], obs[1][0], obs[2][0], obs[0][1], obs[1][1], obs[2][1], oc, x,
      og, w_out_bf16, nf, w_router)


ROUTE_BLOCK = 2 * LANES
ROUTE_ROWS = ROUTE_BLOCK // LANES


def _route_kernel(aff_ref, lpos_ref, cnt_ref, off_ref, base_ref, base_acc, *, cap):
    e = pl.program_id(0)

    @pl.when(e == 0)
    def _():
        base_acc[...] = jnp.zeros(base_acc.shape, F32)

    bits = lax.bitcast_convert_type(aff_ref[0], jnp.int32)
    nb = bits.shape[0]

    def search(i, thr):
        cand = thr | jnp.left_shift(jnp.int32(1), 30 - i)
        cnt = jnp.sum((bits >= cand).astype(F32), keepdims=True)
        return jnp.where(cnt >= cap, cand, thr)

    thr = lax.fori_loop(0, 31, search, jnp.zeros((1, 1), jnp.int32))
    gt = bits > thr
    eq = bits == thr
    need = cap - jnp.sum(gt.astype(F32), keepdims=True)

    ri = lax.broadcasted_iota(jnp.int32, (LANES, LANES), 0)
    ci = lax.broadcasted_iota(jnp.int32, (LANES, LANES), 1)
    upper = (ri <= ci).astype(F32).astype(BF16)
    ones = jnp.ones((LANES, LANES), BF16)
    rr = lax.broadcasted_iota(jnp.int32, (nb, nb), 0)
    rc = lax.broadcasted_iota(jnp.int32, (nb, nb), 1)
    strict_lower = (rc < rr).astype(F32).astype(BF16)
    same_block = (rc // ROUTE_ROWS) == (rr // ROUTE_ROWS)
    block_all = same_block.astype(F32).astype(BF16)
    block_lower = (same_block & (rc < rr)).astype(F32).astype(BF16)

    def prefix(mask_f32):
        mb = mask_f32.astype(BF16)
        in_row = jnp.dot(mb, upper, preferred_element_type=F32)
        tot = jnp.dot(mb, ones, preferred_element_type=F32)
        offs = jnp.dot(strict_lower, tot.astype(BF16), preferred_element_type=F32)
        return in_row, tot, offs

    eq_f = eq.astype(F32)
    in_row_e, _, offs_e = prefix(eq_f)
    rank_eq = offs_e + in_row_e - eq_f
    sel = gt | (eq & (rank_eq < need))
    in_row, tot, offs = prefix(sel.astype(F32))
    tot_b = tot.astype(BF16)
    in_block = jnp.dot(block_lower, tot_b, preferred_element_type=F32)
    block_tot = jnp.dot(block_all, tot_b, preferred_element_type=F32)
    base = base_acc[...]
    lpos_ref[0] = jnp.where(sel, base + in_block + in_row - 1.0, -1.0).astype(jnp.int32)
    cnt_ref[0] = block_tot.astype(jnp.int32)
    off_ref[0] = (offs - in_block).astype(jnp.int32)
    base_ref[0] = base.astype(jnp.int32)
    base_acc[...] = base + block_tot


def route(aff, cap):
    n = aff.shape[0]
    nb = n // LANES
    aff_t = aff.T.reshape(N_EXPERTS, nb, LANES)
    spec = pl.BlockSpec((1, nb, LANES), lambda e: (e, 0, 0))
    outs = pl.pallas_call(
        functools.partial(_route_kernel, cap=cap),
        grid=(N_EXPERTS,),
        in_specs=[spec],
        out_specs=[spec] * 4,
        out_shape=[jax.ShapeDtypeStruct((N_EXPERTS, nb, LANES), jnp.int32)] * 4,
        scratch_shapes=[pltpu.VMEM((nb, LANES), F32)],
        compiler_params=_cparams(("arbitrary",)),
        name="route",
    )(aff_t)
    lpos, cnt, off, base = outs
    nblk = n // ROUTE_BLOCK
    scal = lambda a: a[:, ::ROUTE_ROWS, 0].T.reshape(-1)
    by_block = lambda a: a.reshape(N_EXPERTS, nblk, ROUTE_BLOCK).transpose(1, 0, 2)
    return by_block(lpos), by_block(aff_t), scal(cnt), scal(off), scal(base)


ROW_TILE = (D_MODEL // LANES, LANES)


def _tile_rows(start, rows):
    nf = ROW_TILE[0]
    return pl.ds(pl.multiple_of(start * nf, nf), rows * nf)


def _feature_rows(start, rows, f):
    nf = ROW_TILE[0]
    return pl.ds(start * nf + f, rows, stride=nf)
SEG_PIECE = 16
SEG_BITS = tuple(1 << k for k in range(3, -1, -1))
MAX_BLOCK_ROWS = N_EXPERTS * ROUTE_BLOCK


def _segment_copies(fn, cnt_s, blk, make):
    for e in range(N_EXPERTS):
        c = cnt_s[blk * N_EXPERTS + e]
        npiece = c // SEG_PIECE

        def piece(i, carry):
            fn(make(e, i * SEG_PIECE, SEG_PIECE), e)
            return carry

        lax.fori_loop(0, npiece, piece, 0)
        rem = c - npiece * SEG_PIECE
        for bit in SEG_BITS:
            done = npiece * SEG_PIECE + (rem // (2 * bit)) * (2 * bit)

            @pl.when((rem & bit) != 0)
            def _():
                fn(make(e, done, bit), e)


def _wait_rows(total, make_rows):
    for k in range(MAX_BLOCK_ROWS.bit_length() - 1, -1, -1):
        rows = 1 << k

        @pl.when((total & rows) != 0)
        def _():
            make_rows(rows).wait()


def _start(cp, e):
    cp.start(priority=e % 2)


def _block_rows(cnt_s, base_s, blk):
    last = blk * N_EXPERTS + N_EXPERTS - 1
    return base_s[last] + cnt_s[last]


def _chunk_experts(cnt_s, base_s, blk, r0):
    lo = jnp.int32(0)
    hi = jnp.int32(0)
    for e in range(N_EXPERTS):
        start = base_s[blk * N_EXPERTS + e]
        end = start + cnt_s[blk * N_EXPERTS + e]
        lo = lo + (end <= r0).astype(jnp.int32)
        hi = hi + (start < r0 + LANES).astype(jnp.int32)
    return lo, hi


def _dispatch_kernel(cnt_s, off_s, base_s, h_ref, lpos_ref, xe_hbm, xc_ref, sem, *, cap):
    b = pl.program_id(0)
    nblk = pl.num_programs(0)
    slot = b % 2
    h = h_ref[...]

    def chunk(k, carry):
        r0 = pl.multiple_of(k * LANES, LANES)
        rho = r0 + lax.broadcasted_iota(jnp.int32, (LANES, ROUTE_BLOCK), 0)
        e_lo, e_hi = _chunk_experts(cnt_s, base_s, b, r0)

        def mark(e, hit):
            return jnp.where(lpos_ref[0, pl.ds(e, 1), :] == rho, 1.0, hit)

        hit = lax.fori_loop(e_lo, e_hi, mark, jnp.zeros((LANES, ROUTE_BLOCK), F32))
        onehot = hit.astype(BF16)
        for f in range(ROW_TILE[0]):
            xc_ref[slot, _feature_rows(r0, LANES, f), :] = jnp.dot(
                onehot, h[:, f * LANES:(f + 1) * LANES], preferred_element_type=F32)
        return carry

    lax.fori_loop(0, (_block_rows(cnt_s, base_s, b) + LANES - 1) // LANES, chunk, 0)

    def copies(blk, sl):
        def make(e, done, rows):
            src = base_s[blk * N_EXPERTS + e] + done
            dst = e * cap + off_s[blk * N_EXPERTS + e] + done
            return pltpu.make_async_copy(xc_ref.at[sl, _tile_rows(src, rows)],
                                         xe_hbm.at[_tile_rows(dst, rows)], sem.at[sl])
        return make

    _segment_copies(_start, cnt_s, b, copies(b, slot))

    def drain(sl):
        def make_rows(rows):
            return pltpu.make_async_copy(xc_ref.at[sl, _tile_rows(0, rows)],
                                         xe_hbm.at[_tile_rows(0, rows)], sem.at[sl])
        return make_rows

    @pl.when(b > 0)
    def _():
        _wait_rows(_block_rows(cnt_s, base_s, b - 1), drain(1 - slot))

    @pl.when(b == nblk - 1)
    def _():
        _wait_rows(_block_rows(cnt_s, base_s, b), drain(slot))


def dispatch(hn, lpos_t, cnt, off, base, cap):
    n = hn.shape[0]
    nb = n // ROUTE_BLOCK
    grid_spec = pltpu.PrefetchScalarGridSpec(
        num_scalar_prefetch=3,
        grid=(nb,),
        in_specs=[pl.BlockSpec((ROUTE_BLOCK, D_MODEL), lambda b, *_: (b, 0)),
                  pl.BlockSpec((1, N_EXPERTS, ROUTE_BLOCK), lambda b, *_: (b, 0, 0))],
        out_specs=pl.BlockSpec(memory_space=pl.ANY),
        scratch_shapes=[pltpu.VMEM((2, MAX_BLOCK_ROWS * ROW_TILE[0], LANES), F32),
                        pltpu.SemaphoreType.DMA((2,))],
    )
    return pl.pallas_call(
        functools.partial(_dispatch_kernel, cap=cap),
        grid_spec=grid_spec,
        out_shape=jax.ShapeDtypeStruct((N_EXPERTS * cap * ROW_TILE[0], LANES), F32),
        compiler_params=_cparams(("arbitrary",)),
        name="dispatch",
    )(cnt, off, base, hn, lpos_t)


def _ffn_kernel(xe_ref, wg_ref, wu_ref, wd_ref, ye_ref, xs_ref, *, hid_chunk):
    m = xs_ref.shape[0]
    nf = ROW_TILE[0]
    for f in range(nf):
        xs_ref[:, f * LANES:(f + 1) * LANES] = xe_ref[pl.ds(f, m, stride=nf), :].astype(BF16)
    x = xs_ref[...]
    acc = jnp.zeros((m, D_MODEL), F32)
    for c in range(D_EXPERT // hid_chunk):
        cols = slice(c * hid_chunk, (c + 1) * hid_chunk)
        g = jnp.dot(x, wg_ref[0, :, cols], preferred_element_type=F32)
        u = jnp.dot(x, wu_ref[0, :, cols], preferred_element_type=F32)
        hid = (g * jax.nn.sigmoid(g) * u).astype(BF16)
        acc = acc + jnp.dot(hid, wd_ref[0, cols, :], preferred_element_type=F32)
    for f in range(nf):
        ye_ref[pl.ds(f, m, stride=nf), :] = acc[:, f * LANES:(f + 1) * LANES]


def expert_ffn(xe, wg, wu, wd, cap, tm=512, hid_chunk=512):
    nf = ROW_TILE[0]
    tiles = cap // tm
    io_spec = pl.BlockSpec((tm * nf, LANES), lambda e, j: (e * tiles + j, 0))
    return pl.pallas_call(
        functools.partial(_ffn_kernel, hid_chunk=hid_chunk),
        grid=(N_EXPERTS, tiles),
        in_specs=[io_spec,
                  pl.BlockSpec((1, D_MODEL, D_EXPERT), lambda e, j: (e, 0, 0)),
                  pl.BlockSpec((1, D_MODEL, D_EXPERT), lambda e, j: (e, 0, 0)),
                  pl.BlockSpec((1, D_EXPERT, D_MODEL), lambda e, j: (e, 0, 0))],
        out_specs=io_spec,
        out_shape=jax.ShapeDtypeStruct(xe.shape, F32),
        scratch_shapes=[pltpu.VMEM((tm, D_MODEL), BF16)],
        compiler_params=_cparams(("arbitrary", "arbitrary")),
        name="expert_ffn",
    )(xe, wg, wu, wd)


def _combine_kernel(cnt_s, off_s, base_s, x_ref, lpos_ref, aff_ref, fg_ref, ye_hbm, o_ref,
                    yc_ref, sem, *, cap, final):
    b = pl.program_id(0)
    nblk = pl.num_programs(0)
    slot = b % 2

    def copies(blk, sl):
        def make(e, done, rows):
            src = e * cap + off_s[blk * N_EXPERTS + e] + done
            dst = base_s[blk * N_EXPERTS + e] + done
            return pltpu.make_async_copy(ye_hbm.at[_tile_rows(src, rows)],
                                         yc_ref.at[sl, _tile_rows(dst, rows)], sem.at[sl])
        return make

    @pl.when(b == 0)
    def _():
        yc_ref[...] = jnp.zeros(yc_ref.shape, F32)
        _segment_copies(_start, cnt_s, 0, copies(0, 0))

    @pl.when(b + 1 < nblk)
    def _():
        _segment_copies(_start, cnt_s, b + 1, copies(b + 1, 1 - slot))

    def arrived(rows):
        return pltpu.make_async_copy(ye_hbm.at[_tile_rows(0, rows)],
                                     yc_ref.at[slot, _tile_rows(0, rows)], sem.at[slot])

    _wait_rows(_block_rows(cnt_s, base_s, b), arrived)

    o_ref[...] = x_ref[...]

    def chunk(k, carry):
        r0 = pl.multiple_of(k * LANES, LANES)
        rho = r0 + lax.broadcasted_iota(jnp.int32, (LANES, ROUTE_BLOCK), 0)
        e_lo, e_hi = _chunk_experts(cnt_s, base_s, b, r0)

        def weigh(e, wt):
            return jnp.where(lpos_ref[0, pl.ds(e, 1), :] == rho, aff_ref[0, pl.ds(e, 1), :], wt)

        wt = lax.fori_loop(e_lo, e_hi, weigh, jnp.zeros((LANES, ROUTE_BLOCK), F32))
        w = wt.T
        hi = w.astype(BF16)
        lo = (w - hi.astype(F32)).astype(BF16)
        w2 = jnp.concatenate([hi, lo], axis=1)
        for f in range(ROW_TILE[0]):
            y = yc_ref[slot, _feature_rows(r0, LANES, f), :].astype(BF16)
            o_ref[:, f * LANES:(f + 1) * LANES] += jnp.dot(
                w2, jnp.concatenate([y, y], axis=0), preferred_element_type=F32)
        return carry

    lax.fori_loop(0, (_block_rows(cnt_s, base_s, b) + LANES - 1) // LANES, chunk, 0)
    if final:
        o_ref[...] = _rms(o_ref[...], fg_ref[...])


def combine(x1, lpos_t, aff_t, final_gain, ye, cnt, off, base, cap, final):
    n = x1.shape[0]
    nb = n // ROUTE_BLOCK
    by_block = pl.BlockSpec((1, N_EXPERTS, ROUTE_BLOCK), lambda b, *_: (b, 0, 0))
    grid_spec = pltpu.PrefetchScalarGridSpec(
        num_scalar_prefetch=3,
        grid=(nb,),
        in_specs=[pl.BlockSpec((ROUTE_BLOCK, D_MODEL), lambda b, *_: (b, 0)),
                  by_block, by_block,
                  pl.BlockSpec((1, D_MODEL), lambda b, *_: (0, 0)),
                  pl.BlockSpec(memory_space=pl.ANY)],
        out_specs=pl.BlockSpec((ROUTE_BLOCK, D_MODEL), lambda b, *_: (b, 0)),
        scratch_shapes=[pltpu.VMEM((2, MAX_BLOCK_ROWS * ROW_TILE[0], LANES), F32),
                        pltpu.SemaphoreType.DMA((2,))],
    )
    return pl.pallas_call(
        functools.partial(_combine_kernel, cap=cap, final=final),
        grid_spec=grid_spec,
        out_shape=jax.ShapeDtypeStruct((n, D_MODEL), F32),
        compiler_params=_cparams(("arbitrary",)),
        name="combine",
    )(cnt, off, base, x1, lpos_t, aff_t, final_gain.reshape(1, D_MODEL), ye)


def moe(x1, hn, aff, wg, wu, wd, final_gain, final):
    n = x1.shape[0]
    cap = EC_CAPACITY * n // N_EXPERTS
    lpos_t, aff_t, cnt, off, base = route(aff, cap)
    xe = dispatch(hn, lpos_t, cnt, off, base, cap)
    ye = expert_ffn(xe, wg, wu, wd, cap)
    return combine(x1, lpos_t, aff_t, final_gain, ye, cnt, off, base, cap, final)


def _trunk(x, p):
    batch, seq, _ = x.shape
    n = batch * seq
    x = x.reshape(n, D_MODEL)
    for l in range(DEPTH):
        a_arr, c_arr, *groups = in_proj(x, p["norm_mix"][l], p["w_in"][l], batch, seq)
        qx, k2, vx = mixer_a_prep(a_arr, p["q_gain"][l], p["k_gain"][l], batch, seq)
        oa = mixer_a(qx, k2, vx, batch, seq)
        obs = [mixer_b_group(groups[g], p["rel_bias"], g) for g in range(B_GROUPS)]
        oc = mixer_c(c_arr, p["na_rpb"][l], batch, seq)
        x1, hn, aff = out_proj(oa, obs, oc, x, p["out_gain"][l], p["w_out"][l],
                               p["norm_ffn"][l], p["w_router"][l], seq)
        x = moe(x1, hn, aff, p["w_gate"][l], p["w_up"][l], p["w_down"][l],
                p["final_norm"], final=(l == DEPTH - 1))
    return x.reshape(batch, seq, D_MODEL)


def kernel(x_prompt, x_sample, w_in, w_out, norm_mix, norm_ffn, q_gain, k_gain, out_gain,
           na_rpb, rel_bias, w_router, w_gate, w_up, w_down, final_norm):
    p = dict(w_in=permute_w_in(w_in).astype(BF16), w_out=w_out.astype(BF16), norm_mix=norm_mix,
             norm_ffn=norm_ffn, q_gain=q_gain, k_gain=k_gain, out_gain=out_gain,
             na_rpb=na_rpb, rel_bias=rel_bias, w_router=w_router,
             w_gate=w_gate.astype(BF16), w_up=w_up.astype(BF16), w_down=w_down.astype(BF16),
             final_norm=final_norm)
    return (_trunk(x_prompt, p), _trunk(x_sample, p))
```
